```python
import jax, jax.numpy as jnp
from jax import lax
import numpy as np

D_MODEL = 1024
BATCH = 4
SEQ = 8192
DEPTH = 1

PLE_DIM = 256
CONV_CH = D_MODEL // 2
CONV_WIDTH = 31
HEAD_DIM = 64
N_HEADS = (D_MODEL - CONV_CH) // HEAD_DIM
N_KV = 2
HPG = N_HEADS // N_KV
NSA_W = N_HEADS * HEAD_DIM
KV_W = N_KV * HEAD_DIM
CMP_LEN = 32
CMP_STRIDE = 16
CMP_HIDDEN = 256
SLC_BLOCK = 64
SLC_TOPN = 16
WINDOW = 512
Q_BLOCK = 128
D_FF = 4 * D_MODEL
N_BRANCH = 3
COL_SIZES = [2 * CONV_CH, NSA_W] + [KV_W] * 6 + [N_BRANCH * N_HEADS]
D_IN = sum(COL_SIZES)
LN_EPS = 1e-5
NEG_INF = -1e30
FORCE_BONUS = 1e4

kernel_name = "hymba_conformer_nsa_deepnorm_layer"


def layer_norm(x, g, b):
    xf = x.astype(jnp.float32)
    mu = xf.mean(-1, keepdims=True)
    var = jnp.square(xf - mu).mean(-1, keepdims=True)
    return ((xf - mu) * lax.rsqrt(var + LN_EPS) * g.astype(jnp.float32) + b.astype(jnp.float32)).astype(x.dtype)


def conformer_conv(u, dw_w, dw_b, ln_g, ln_b):
    a, g = jnp.split(u, 2, axis=-1)
    v = a * jax.nn.sigmoid(g)
    y = lax.conv_general_dilated(v, dw_w, window_strides=(1,), padding=[(CONV_WIDTH - 1, 0)],
                                 dimension_numbers=('NWC', 'WIO', 'NWC'),
                                 feature_group_count=CONV_CH) + dw_b
    return jax.nn.silu(layer_norm(y, ln_g, ln_b))


def compress_blocks(k, pos_emb, w1, w2):
    B, S, G, Dh = k.shape
    n_cmp = (S - CMP_LEN) // CMP_STRIDE + 1
    idx = np.arange(n_cmp)[:, None] * CMP_STRIDE + np.arange(CMP_LEN)[None, :]
    blk = k[:, idx] + pos_emb[:, None, :]
    blk = blk.transpose(0, 3, 1, 2, 4).reshape(B, G, n_cmp, CMP_LEN * Dh)
    return jax.nn.silu(blk @ w1) @ w2


def cmp_to_slc_matrix(n_cmp, n_slc):
    c0 = np.arange(n_cmp) * CMP_STRIDE
    c1 = c0 + CMP_LEN - 1
    s0 = np.arange(n_slc) * SLC_BLOCK
    s1 = s0 + SLC_BLOCK - 1
    return ((c0[:, None] <= s1[None, :]) & (c1[:, None] >= s0[None, :])).astype(np.float32)


def masked_softmax(s, mask):
    return jax.nn.softmax(jnp.where(mask, s, NEG_INF), axis=-1)


def nsa_attention(q, kc, vc, ks, vs, kw, vw, gates):
    B, G, H, S, Dh = q.shape
    dt = q.dtype
    n_cmp = kc.shape[2]
    n_slc = S // SLC_BLOCK
    top_n = min(SLC_TOPN, n_slc)
    scale = HEAD_DIM ** -0.5
    m_overlap = jnp.asarray(cmp_to_slc_matrix(n_cmp, n_slc))
    cmp_end = jnp.arange(n_cmp) * CMP_STRIDE + CMP_LEN - 1
    blk_id = jnp.arange(n_slc)
    ks_blk = ks.reshape(B, G, n_slc, SLC_BLOCK, Dh)
    vs_blk = vs.reshape(B, G, n_slc, SLC_BLOCK, Dh)
    pad = ((0, 0), (0, 0), (WINDOW, 0), (0, 0))
    kw_pad = jnp.pad(kw, pad)
    vw_pad = jnp.pad(vw, pad)
    b_idx = jnp.arange(B)[:, None, None, None]
    g_idx = jnp.arange(G)[None, :, None, None]

    def block(qb):
        t0 = qb * Q_BLOCK
        qq = lax.dynamic_slice_in_dim(q, t0, Q_BLOCK, axis=3)
        gg = jax.nn.sigmoid(lax.dynamic_slice_in_dim(gates, t0, Q_BLOCK, axis=3).astype(jnp.float32))
        pos = t0 + jnp.arange(Q_BLOCK)
        s_c = jnp.einsum('bghqd,bgnd->bghqn', qq, kc).astype(jnp.float32) * scale
        p_c = masked_softmax(s_c, cmp_end[None, :] <= pos[:, None])
        p_c = jnp.where((pos >= CMP_LEN - 1)[:, None], p_c, 0.0)
        o_c = jnp.einsum('bghqn,bgnd->bghqd', p_c.astype(dt), vc)
        imp = p_c.sum(axis=2) @ m_overlap
        cur = pos // SLC_BLOCK
        forced = (blk_id[None, :] == 0) | (blk_id[None, :] == cur[:, None]) | (blk_id[None, :] == cur[:, None] - 1)
        valid_b = blk_id[None, :] * SLC_BLOCK <= pos[:, None]
        score = jnp.where(valid_b, imp + jnp.where(forced, FORCE_BONUS, 0.0), NEG_INF)
        _, sel = lax.top_k(score, top_n)
        k_sel = ks_blk[b_idx, g_idx, sel].reshape(B, G, Q_BLOCK, top_n * SLC_BLOCK, Dh)
        v_sel = vs_blk[b_idx, g_idx, sel].reshape(B, G, Q_BLOCK, top_n * SLC_BLOCK, Dh)
        kpos = (sel[..., None] * SLC_BLOCK + jnp.arange(SLC_BLOCK)).reshape(B, G, Q_BLOCK, top_n * SLC_BLOCK)
        s_s = jnp.einsum('bghqd,bgqkd->bghqk', qq, k_sel).astype(jnp.float32) * scale
        p_s = masked_softmax(s_s, (kpos <= pos[None, None, :, None])[:, :, None])
        o_s = jnp.einsum('bghqk,bgqkd->bghqd', p_s.astype(dt), v_sel)
        k_win = lax.dynamic_slice_in_dim(kw_pad, t0, WINDOW + Q_BLOCK, axis=2)
        v_win = lax.dynamic_slice_in_dim(vw_pad, t0, WINDOW + Q_BLOCK, axis=2)
        kpos_w = t0 - WINDOW + jnp.arange(WINDOW + Q_BLOCK)
        valid_w = (kpos_w[None, :] <= pos[:, None]) & (kpos_w[None, :] > pos[:, None] - WINDOW) & (kpos_w[None, :] >= 0)
        s_w = jnp.einsum('bghqd,bgkd->bghqk', qq, k_win).astype(jnp.float32) * scale
        p_w = masked_softmax(s_w, valid_w)
        o_w = jnp.einsum('bghqk,bgkd->bghqd', p_w.astype(dt), v_win)
        o = gg[..., 0:1] * o_c + gg[..., 1:2] * o_s + gg[..., 2:3] * o_w
        return o.astype(dt)

    out = lax.map(block, jnp.arange(S // Q_BLOCK))
    return out.transpose(1, 0, 4, 2, 3, 5).reshape(B, S, G * H * Dh)


def setup_inputs(seed: int = 0) -> dict:
    key = jax.random.key(seed)
    ks = iter(jax.random.split(key, 40))
    beta = (8 * DEPTH) ** -0.25
    f32 = jnp.float32

    def nrm(shape, scale):
        return jax.random.normal(next(ks), shape, f32) * scale

    def gain(shape):
        return 1.0 + nrm(shape, 0.02)

    L = DEPTH
    return {
        "x": nrm((BATCH, SEQ, D_MODEL), 1.0),
        "p": nrm((DEPTH, BATCH, SEQ, PLE_DIM), 1.0),
        "w_in": nrm((L, D_MODEL, D_IN), D_MODEL ** -0.5),
        "b_in": nrm((L, D_IN), 0.02),
        "conv_dw_w": nrm((L, CONV_WIDTH, 1, CONV_CH), CONV_WIDTH ** -0.5),
        "conv_dw_b": nrm((L, CONV_CH), 0.02),
        "conv_ln_g": gain((L, CONV_CH)),
        "conv_ln_b": nrm((L, CONV_CH), 0.02),
        "cmp_pos_k": nrm((L, CMP_LEN, HEAD_DIM), 0.1),
        "cmp_w1_k": nrm((L, CMP_LEN * HEAD_DIM, CMP_HIDDEN), (CMP_LEN * HEAD_DIM) ** -0.5),
        "cmp_w2_k": nrm((L, CMP_HIDDEN, HEAD_DIM), CMP_HIDDEN ** -0.5),
        "cmp_pos_v": nrm((L, CMP_LEN, HEAD_DIM), 0.1),
        "cmp_w1_v": nrm((L, CMP_LEN * HEAD_DIM, CMP_HIDDEN), (CMP_LEN * HEAD_DIM) ** -0.5),
        "cmp_w2_v": nrm((L, CMP_HIDDEN, HEAD_DIM), CMP_HIDDEN ** -0.5),
        "w_out": nrm((L, D_MODEL, D_MODEL), D_MODEL ** -0.5 * beta),
        "b_out": nrm((L, D_MODEL), 0.02),
        "ln1_g": gain((L, D_MODEL)),
        "ln1_b": nrm((L, D_MODEL), 0.02),
        "w_up": nrm((L, D_MODEL, D_FF), D_MODEL ** -0.5),
        "b_up": nrm((L, D_FF), 0.02),
        "w_down": nrm((L, D_FF, D_MODEL), D_FF ** -0.5 * beta),
        "b_down": nrm((L, D_MODEL), 0.02),
        "w_pe": nrm((L, PLE_DIM, D_MODEL), PLE_DIM ** -0.5 * beta),
        "w_pg": nrm((L, D_MODEL, D_MODEL), D_MODEL ** -0.5),
        "ln2_g": gain((L, D_MODEL)),
        "ln2_b": nrm((L, D_MODEL), 0.02),
    }


def reference(x, p, w_in, b_in, conv_dw_w, conv_dw_b, conv_ln_g, conv_ln_b,
              cmp_pos_k, cmp_w1_k, cmp_w2_k, cmp_pos_v, cmp_w1_v, cmp_w2_v,
              w_out, b_out, ln1_g, ln1_b, w_up, b_up, w_down, b_down,
              w_pe, w_pg, ln2_g, ln2_b):
    alpha = (2 * DEPTH) ** 0.25
    B, S, _ = x.shape
    split_at = np.cumsum(COL_SIZES)[:-1].tolist()
    for i in range(DEPTH):
        h = x @ w_in[i] + b_in[i]
        u_conv, q, kc, vc, ksl, vsl, kwn, vwn, gt = jnp.split(h, split_at, axis=-1)
        y_conv = conformer_conv(u_conv, conv_dw_w[i], conv_dw_b[i], conv_ln_g[i], conv_ln_b[i])
        q = q.reshape(B, S, N_KV, HPG, HEAD_DIM).transpose(0, 2, 3, 1, 4)
        to_kv = lambda t: t.reshape(B, S, N_KV, HEAD_DIM)
        kc_t = compress_blocks(to_kv(kc), cmp_pos_k[i], cmp_w1_k[i], cmp_w2_k[i])
        vc_t = compress_blocks(to_kv(vc), cmp_pos_v[i], cmp_w1_v[i], cmp_w2_v[i])
        heads = lambda t: to_kv(t).transpose(0, 2, 1, 3)
        gt = gt.reshape(B, S, N_KV, HPG, N_BRANCH).transpose(0, 2, 3, 1, 4)
        y_nsa = nsa_attention(q, kc_t, vc_t, heads(ksl), heads(vsl), heads(kwn), heads(vwn), gt)
        mix = jnp.concatenate([y_conv, y_nsa], axis=-1) @ w_out[i] + b_out[i]
        x = layer_norm(alpha * x + mix, ln1_g[i], ln1_b[i])
        ff = jnp.square(jax.nn.relu(x @ w_up[i] + b_up[i])) @ w_down[i] + b_down[i]
        ple = (p[i] @ w_pe[i]) * jax.nn.sigmoid(x @ w_pg[i])
        x = layer_norm(alpha * x + ff + ple, ln2_g[i], ln2_b[i])
    return x
```

```python
import functools

import jax
import jax.numpy as jnp
import numpy as np
from jax import lax
from jax.experimental import pallas as pl
from jax.experimental.pallas import tpu as pltpu

F32 = jnp.float32
BF16 = jnp.bfloat16

D_MODEL = 1024
PLE_DIM = 256
CONV_CH = 512
CONV_WIDTH = 31
HEAD_DIM = 64
N_KV = 2
HPG = 4
N_BRANCH = 3
CMP_LEN = 32
CMP_STRIDE = 16
CMP_HIDDEN = 256
SLC_BLOCK = 64
SLC_TOPN = 16
WINDOW = 512
D_FF = 4 * D_MODEL
LN_EPS = 1e-5
NEG_INF = -1e30
FORCE_BONUS = 1e4
DEPTH = 1
ALPHA = (2 * DEPTH) ** 0.25

UNSELECTED_BIAS = -float(2 ** 30)
BELOW_NEG_INF = -3e38

PROJ_ROWS = 512
CONV_ROWS = 256
CONV_HALO = 32
CONV_ROW_CHUNK = 32
Q_TILE = 128
KEY_CHUNK = 512
WIN_UNIT = 128
OUT_ROWS = 512
FFN_ROWS = 512
FFN_CHUNK = 1024
VMEM_LIMIT = 56 * 1024 * 1024

RM_COLS = 1792
T_ROWS = 800


def _layer_norm(z, g, b):
    mu = jnp.mean(z, axis=-1, keepdims=True)
    zc = z - mu
    var = jnp.mean(zc * zc, axis=-1, keepdims=True)
    return zc * lax.rsqrt(var + LN_EPS) * g + b


def _dot(a, b):
    return jnp.dot(a, b, preferred_element_type=F32)


def _dot_nt(a, b):
    return lax.dot_general(a, b, (((1,), (1,)), ((), ())), preferred_element_type=F32)


def _dot_tn(a, b):
    return lax.dot_general(a, b, (((0,), (0,)), ((), ())), preferred_element_type=F32)


def _proj_kernel(x_ref, wrm_ref, brm_ref, wt_ref, bt_ref,
                 vglu_ref, kc_ref, vc_ref, ks_ref, kw_ref,
                 qT_ref, vsT_ref, vwT_ref, gT_ref):
    xb = x_ref[0].astype(BF16)

    def rm(c0, c1):
        return _dot(xb, wrm_ref[:, c0:c1]) + brm_ref[:, c0:c1]

    half = CONV_CH // 2
    for c in range(0, CONV_CH, half):
        a = rm(c, c + half)
        g = rm(CONV_CH + c, CONV_CH + c + half)
        vglu_ref[0, :, c:c + half] = a * jax.nn.sigmoid(g)

    kv = rm(1024, 1280)
    kc_ref[0, 0] = kv[:, 0:64]
    kc_ref[0, 1] = kv[:, 64:128]
    vc_ref[0, 0] = kv[:, 128:192]
    vc_ref[0, 1] = kv[:, 192:256]

    kz = rm(1280, 1792).astype(BF16)
    ks_ref[0, 0] = kz[:, 0:128]
    ks_ref[0, 1] = kz[:, 128:256]
    kw_ref[0, 0] = kz[:, 256:384]
    kw_ref[0, 1] = kz[:, 384:512]

    def tr(r0, r1):
        return _dot_nt(wt_ref[r0:r1, :], xb) + bt_ref[r0:r1, :]

    qT_ref[0] = tr(0, 512).astype(BF16)
    vsT_ref[0, 0] = tr(512, 640).astype(BF16)
    vw = tr(640, 768).astype(BF16)
    for u in range(PROJ_ROWS // WIN_UNIT):
        vwT_ref[0, u] = vw[:, u * WIN_UNIT:(u + 1) * WIN_UNIT]
    gT_ref[0] = jax.nn.sigmoid(tr(768, 800))


def _proj_call(x, wrm, brm, wt, bt):
    B, S, D = x.shape
    tm = PROJ_ROWS
    n = S // tm
    const = lambda b, i: (0, 0)
    out_shape = (
        jax.ShapeDtypeStruct((B, S, CONV_CH), F32),
        jax.ShapeDtypeStruct((B, N_KV, S, HEAD_DIM), F32),
        jax.ShapeDtypeStruct((B, N_KV, S, HEAD_DIM), F32),
        jax.ShapeDtypeStruct((B, N_KV, S, 128), BF16),
        jax.ShapeDtypeStruct((B, N_KV, S, 128), BF16),
        jax.ShapeDtypeStruct((B, 512, S), BF16),
        jax.ShapeDtypeStruct((B, n, 128, tm), BF16),
        jax.ShapeDtypeStruct((B, S // WIN_UNIT, 128, WIN_UNIT), BF16),
        jax.ShapeDtypeStruct((B, 32, S), F32),
    )
    kvspec = pl.BlockSpec((1, N_KV, tm, HEAD_DIM), lambda b, i: (b, 0, i, 0))
    kzspec = pl.BlockSpec((1, N_KV, tm, 128), lambda b, i: (b, 0, i, 0))
    out_specs = (
        pl.BlockSpec((1, tm, CONV_CH), lambda b, i: (b, i, 0)),
        kvspec, kvspec, kzspec, kzspec,
        pl.BlockSpec((1, 512, tm), lambda b, i: (b, 0, i)),
        pl.BlockSpec((1, 1, 128, tm), lambda b, i: (b, i, 0, 0)),
        pl.BlockSpec((1, tm // WIN_UNIT, 128, WIN_UNIT), lambda b, i: (b, i, 0, 0)),
        pl.BlockSpec((1, 32, tm), lambda b, i: (b, 0, i)),
    )
    in_specs = [
        pl.BlockSpec((1, tm, D), lambda b, i: (b, i, 0)),
        pl.BlockSpec((D, RM_COLS), const),
        pl.BlockSpec((1, RM_COLS), const),
        pl.BlockSpec((T_ROWS, D), const),
        pl.BlockSpec((T_ROWS, 1), const),
    ]
    return pl.pallas_call(
        _proj_kernel, grid=(B, n), in_specs=in_specs, out_specs=out_specs,
        out_shape=out_shape, name="proj",
        compiler_params=pltpu.CompilerParams(
            dimension_semantics=("parallel", "parallel"), vmem_limit_bytes=VMEM_LIMIT),
    )(x, wrm, brm, wt, bt)


def _conv_kernel(cur_ref, halo_ref, w_ref, b_ref, g_ref, beta_ref, o_ref, xs_ref):
    i = pl.program_id(1)
    xs_ref[0:CONV_HALO, :] = jnp.where(i > 0, halo_ref[0], 0.0)
    xs_ref[CONV_HALO:, :] = cur_ref[0]
    lead = CONV_HALO - (CONV_WIDTH - 1)
    for r in range(0, CONV_ROWS, CONV_ROW_CHUNK):
        acc = jnp.zeros((CONV_ROW_CHUNK, CONV_CH), F32)
        for k in range(CONV_WIDTH):
            acc = acc + xs_ref[r + lead + k:r + lead + k + CONV_ROW_CHUNK, :] * w_ref[k:k + 1, :]
        y = _layer_norm(acc + b_ref[...], g_ref[...], beta_ref[...])
        o_ref[0, r:r + CONV_ROW_CHUNK, :] = (y * jax.nn.sigmoid(y)).astype(BF16)


def _conv_call(v, w, b, g, beta):
    B, S, C = v.shape
    tc = CONV_ROWS
    ratio = tc // CONV_HALO
    const = lambda b_, i: (0, 0)
    return pl.pallas_call(
        _conv_kernel, grid=(B, S // tc),
        in_specs=[
            pl.BlockSpec((1, tc, C), lambda b_, i: (b_, i, 0)),
            pl.BlockSpec((1, CONV_HALO, C), lambda b_, i: (b_, jnp.maximum(i * ratio - 1, 0), 0)),
            pl.BlockSpec((32, C), const),
            pl.BlockSpec((1, C), const), pl.BlockSpec((1, C), const), pl.BlockSpec((1, C), const),
        ],
        out_specs=pl.BlockSpec((1, tc, C), lambda b_, i: (b_, i, 0)),
        out_shape=jax.ShapeDtypeStruct((B, S, C), BF16),
        scratch_shapes=[pltpu.VMEM((tc + CONV_HALO, C), F32)],
        name="conv",
        compiler_params=pltpu.CompilerParams(
            dimension_semantics=("parallel", "parallel"), vmem_limit_bytes=VMEM_LIMIT),
    )(v, v, w, b, g, beta)


def _compress_hidden(r, pos_ref, w1_ref):
    half = CMP_STRIDE * HEAD_DIM
    n_rows = r.shape[0]
    a = _dot((r + pos_ref[0:1, :]).astype(BF16), w1_ref[0:half, :])
    b = _dot((r + pos_ref[1:2, :]).astype(BF16), w1_ref[half:2 * half, :])
    h = a + pltpu.roll(b, n_rows - 1, axis=0)
    return (h * jax.nn.sigmoid(h)).astype(BF16)


def _compress_kernel(rk_ref, rv_ref, pk_ref, w1k_ref, w2k_ref, pv_ref, w1v_ref, w2vT_ref,
                     kc_ref, vcT_ref):
    hk = _compress_hidden(rk_ref[0, 0], pk_ref, w1k_ref)
    kc_ref[0, 0] = _dot(hk, w2k_ref[...]).astype(BF16)
    hv = _compress_hidden(rv_ref[0, 0], pv_ref, w1v_ref)
    vcT_ref[0, 0] = _dot_nt(w2vT_ref[...], hv).astype(BF16)


def _compress_call(rk, rv, pk, w1k, w2k, pv, w1v, w2vT):
    B, G, NC, W = rk.shape
    const = lambda b, g: (0, 0)
    rspec = pl.BlockSpec((1, 1, NC, W), lambda b, g: (b, g, 0, 0))
    return pl.pallas_call(
        _compress_kernel, grid=(B, G),
        in_specs=[
            rspec, rspec,
            pl.BlockSpec((2, W), const), pl.BlockSpec((2 * W, CMP_HIDDEN), const),
            pl.BlockSpec((CMP_HIDDEN, 128), const),
            pl.BlockSpec((2, W), const), pl.BlockSpec((2 * W, CMP_HIDDEN), const),
            pl.BlockSpec((HEAD_DIM, CMP_HIDDEN), const),
        ],
        out_specs=(
            pl.BlockSpec((1, 1, NC, 128), lambda b, g: (b, g, 0, 0)),
            pl.BlockSpec((1, 1, HEAD_DIM, NC), lambda b, g: (b, g, 0, 0)),
        ),
        out_shape=(
            jax.ShapeDtypeStruct((B, G, NC, 128), BF16),
            jax.ShapeDtypeStruct((B, G, HEAD_DIM, NC), BF16),
        ),
        name="compress",
        compiler_params=pltpu.CompilerParams(
            dimension_semantics=("parallel", "parallel"), vmem_limit_bytes=VMEM_LIMIT),
    )(rk, rv, pk, w1k, w2k, pv, w1v, w2vT)


def _nsa_kernel(q_ref, g_ref, kc_ref, vcT_ref, ks_ref, vsT_ref, kw_ref, vwT_ref,
                e_ref, mov_ref, o_ref):
    T = Q_TILE
    R = HPG * T
    C = KEY_CHUNK
    grp = pl.program_id(1)
    qb = pl.program_id(2)
    t0 = qb * T
    n_cmp = kc_ref.shape[2]
    n_slc = mov_ref.shape[0]

    q4 = q_ref[0]
    qT = jnp.concatenate([q4[h * HEAD_DIM:(h + 1) * HEAD_DIM, :] for h in range(HPG)], axis=1)
    zpad = jnp.zeros((HEAD_DIM, R), BF16)
    q1 = jnp.concatenate([qT, zpad], axis=0)
    lane = lax.broadcasted_iota(jnp.int32, (1, R), 1)
    pos = t0 + (lane & (T - 1))

    sc = _dot(kc_ref[0, 0], q1)
    n_id = lax.broadcasted_iota(jnp.int32, (n_cmp, R), 0)
    cmask = (n_id * CMP_STRIDE + (CMP_LEN - 1)) <= pos
    m_c = jnp.max(jnp.where(cmask, sc, NEG_INF), axis=0, keepdims=True)
    p_c = jnp.where(cmask, jnp.exp(sc - m_c), 0.0)
    l_c = jnp.sum(p_c, axis=0, keepdims=True)
    p_c = p_c * (1.0 / jnp.maximum(l_c, 1e-30))
    o_c = _dot(vcT_ref[0, 0], p_c.astype(BF16))

    psum = p_c[:, 0:T]
    for h in range(1, HPG):
        psum = psum + p_c[:, h * T:(h + 1) * T]
    mov = mov_ref[...]
    p_hi = psum.astype(BF16)
    rem = psum - p_hi.astype(F32)
    p_mid = rem.astype(BF16)
    p_lo = (rem - p_mid.astype(F32)).astype(BF16)
    imp = _dot(mov, p_hi) + _dot(mov, p_mid) + _dot(mov, p_lo)

    blk = lax.broadcasted_iota(jnp.int32, (n_slc, T), 0)
    blk_f = blk.astype(F32)
    pos_t = pos[:, 0:T]
    cur = jnp.right_shift(pos_t, 6)
    forced = (blk == 0) | (blk == cur) | (blk == cur - 1)
    valid_b = (blk * SLC_BLOCK) <= pos_t
    work = jnp.where(valid_b, imp + jnp.where(forced, FORCE_BONUS, 0.0), NEG_INF)
    sel = jnp.zeros((n_slc, T), F32)
    for _ in range(SLC_TOPN):
        mx = jnp.max(work, axis=0, keepdims=True)
        first = jnp.min(jnp.where(work == mx, blk_f, float(n_slc)), axis=0, keepdims=True)
        pick = blk_f == first
        sel = jnp.where(pick, 1.0, sel)
        work = jnp.where(pick, BELOW_NEG_INF, work)
    unsel = jnp.where(sel > 0.0, 0.0, UNSELECTED_BIAS).astype(BF16)
    q2 = jnp.concatenate([qT, zpad] + [jnp.concatenate([unsel] * HPG, axis=1)], axis=0)

    def slc_scores(c):
        off = pl.multiple_of(c * C, C)
        k2 = jnp.concatenate([ks_ref[0, 0, pl.ds(off, C), :], e_ref[pl.ds(off, C), :]], axis=1)
        return _dot(k2, q2)

    n_full = t0 // C
    s = slc_scores(n_full)
    kpos = n_full * C + lax.broadcasted_iota(jnp.int32, (C, R), 0)
    s = jnp.where(kpos <= pos, s, NEG_INF)
    m_s = jnp.max(s, axis=0, keepdims=True)
    p = jnp.exp(s - m_s)
    l_s = jnp.sum(p, axis=0, keepdims=True)
    acc = _dot(vsT_ref[0, n_full], p.astype(BF16))

    def slc_body(c, carry):
        m_old, l_old, acc_old = carry
        s_ = slc_scores(c)
        m_new = jnp.maximum(m_old, jnp.max(s_, axis=0, keepdims=True))
        a = jnp.exp(m_old - m_new)
        p_ = jnp.exp(s_ - m_new)
        l_new = a * l_old + jnp.sum(p_, axis=0, keepdims=True)
        acc_new = a * acc_old + _dot(vsT_ref[0, c], p_.astype(BF16))
        return m_new, l_new, acc_new

    m_s, l_s, acc = lax.fori_loop(0, n_full, slc_body, (m_s, l_s, acc))
    o_s = acc * (1.0 / l_s)

    w_keys = WINDOW + T
    start = pl.multiple_of(jnp.maximum(t0 - WINDOW, 0), WIN_UNIT)
    sw = _dot(kw_ref[0, 0, pl.ds(start, w_keys), :], q1)
    dist = pos - (start + lax.broadcasted_iota(jnp.int32, (w_keys, R), 0))
    sw = jnp.where((dist >= 0) & (dist < WINDOW), sw, NEG_INF)
    m_w = jnp.max(sw, axis=0, keepdims=True)
    p_w = jnp.exp(sw - m_w)
    l_w = jnp.sum(p_w, axis=0, keepdims=True)
    u0 = start // WIN_UNIT
    v_win = jnp.concatenate([vwT_ref[0, u0 + u] for u in range(w_keys // WIN_UNIT)], axis=1)
    o_w = _dot(v_win, p_w.astype(BF16)) * (1.0 / l_w)

    def gate(branch):
        rows = [g_ref[0, pl.ds(grp * (HPG * N_BRANCH) + h * N_BRANCH + branch, 1), :]
                for h in range(HPG)]
        return jnp.concatenate(rows, axis=1)

    out = gate(0) * o_c + gate(1) * o_s + gate(2) * o_w
    for h in range(HPG):
        o_ref[0, h * HEAD_DIM:(h + 1) * HEAD_DIM, :] = out[:, h * T:(h + 1) * T].astype(BF16)


def _nsa_call(qT, gT, kc, vcT, ks, vsT, kw, vwT, e_mat, movT):
    B, _, S = qT.shape
    G = N_KV
    T = Q_TILE
    n_cmp = kc.shape[2]
    rows = HPG * HEAD_DIM
    const = lambda b, g, i: (0, 0)
    in_specs = [
        pl.BlockSpec((1, rows, T), lambda b, g, i: (b, g, i)),
        pl.BlockSpec((1, 32, T), lambda b, g, i: (b, 0, i)),
        pl.BlockSpec((1, 1, n_cmp, 128), lambda b, g, i: (b, g, 0, 0)),
        pl.BlockSpec((1, 1, HEAD_DIM, n_cmp), lambda b, g, i: (b, g, 0, 0)),
        pl.BlockSpec((1, 1, S, 128), lambda b, g, i: (b, g, 0, 0)),
        pl.BlockSpec((1, S // KEY_CHUNK, HEAD_DIM, KEY_CHUNK), lambda b, g, i: (b, 0, g, 0)),
        pl.BlockSpec((1, 1, S, 128), lambda b, g, i: (b, g, 0, 0)),
        pl.BlockSpec((1, S // WIN_UNIT, HEAD_DIM, WIN_UNIT), lambda b, g, i: (b, 0, g, 0)),
        pl.BlockSpec((S, 128), const),
        pl.BlockSpec((S // SLC_BLOCK, n_cmp), const),
    ]
    return pl.pallas_call(
        _nsa_kernel, grid=(B, G, S // T), in_specs=in_specs,
        out_specs=pl.BlockSpec((1, rows, T), lambda b, g, i: (b, g, i)),
        out_shape=jax.ShapeDtypeStruct((B, G * rows, S), BF16),
        name="nsa",
        compiler_params=pltpu.CompilerParams(
            dimension_semantics=("parallel", "parallel", "arbitrary"),
            vmem_limit_bytes=VMEM_LIMIT),
    )(qT, gT, kc, vcT, ks, vsT, kw, vwT, e_mat, movT)


def _outproj_kernel(yc_ref, ynT_ref, x_ref, wc_ref, wn_ref, b_ref, g_ref, beta_ref, o_ref):
    mix = _dot(yc_ref[0], wc_ref[...]) + _dot_tn(ynT_ref[0], wn_ref[...])
    z = ALPHA * x_ref[0] + mix + b_ref[...]
    o_ref[0] = _layer_norm(z, g_ref[...], beta_ref[...])


def _outproj_call(yc, ynT, x, wc, wn, b, g, beta):
    B, S, D = x.shape
    tm = OUT_ROWS
    const = lambda b_, i: (0, 0)
    vec = pl.BlockSpec((1, D), const)
    return pl.pallas_call(
        _outproj_kernel, grid=(B, S // tm),
        in_specs=[
            pl.BlockSpec((1, tm, CONV_CH), lambda b_, i: (b_, i, 0)),
            pl.BlockSpec((1, D - CONV_CH, tm), lambda b_, i: (b_, 0, i)),
            pl.BlockSpec((1, tm, D), lambda b_, i: (b_, i, 0)),
            pl.BlockSpec((CONV_CH, D), const), pl.BlockSpec((D - CONV_CH, D), const),
            vec, vec, vec,
        ],
        out_specs=pl.BlockSpec((1, tm, D), lambda b_, i: (b_, i, 0)),
        out_shape=jax.ShapeDtypeStruct((B, S, D), F32),
        name="outproj",
        compiler_params=pltpu.CompilerParams(
            dimension_semantics=("parallel", "parallel"), vmem_limit_bytes=VMEM_LIMIT),
    )(yc, ynT, x, wc, wn, b, g, beta)


def _ffn_kernel(x_ref, p_ref, wup_ref, bup_ref, wdn_ref, bdn_ref, wpe_ref, wpg_ref,
                g_ref, beta_ref, o_ref, acc_ref):
    x1 = x_ref[...]
    xb = x1.astype(BF16)
    ple = _dot(p_ref[...].astype(BF16), wpe_ref[...]) * jax.nn.sigmoid(_dot(xb, wpg_ref[...]))
    acc_ref[...] = ALPHA * x1 + ple + bdn_ref[...]
    for c in range(0, D_FF, FFN_CHUNK):
        u = _dot(xb, wup_ref[:, c:c + FFN_CHUNK]) + bup_ref[:, c:c + FFN_CHUNK]
        u = jnp.square(jnp.maximum(u, 0.0)).astype(BF16)
        acc_ref[...] += _dot(u, wdn_ref[c:c + FFN_CHUNK, :])
    o_ref[...] = _layer_norm(acc_ref[...], g_ref[...], beta_ref[...])


def _ffn_call(x1, p, wup, bup, wdn, bdn, wpe, wpg, g, beta):
    N, D = x1.shape
    tm = FFN_ROWS
    const = lambda i: (0, 0)
    single = pl.Buffered(1)
    vec = pl.BlockSpec((1, D), const)
    return pl.pallas_call(
        _ffn_kernel, grid=(N // tm,),
        in_specs=[
            pl.BlockSpec((tm, D), lambda i: (i, 0)),
            pl.BlockSpec((tm, PLE_DIM), lambda i: (i, 0)),
            pl.BlockSpec((D, D_FF), const, pipeline_mode=single),
            pl.BlockSpec((1, D_FF), const),
            pl.BlockSpec((D_FF, D), const, pipeline_mode=single),
            vec,
            pl.BlockSpec((PLE_DIM, D), const, pipeline_mode=single),
            pl.BlockSpec((D, D), const, pipeline_mode=single),
            vec, vec,
        ],
        out_specs=pl.BlockSpec((tm, D), lambda i: (i, 0)),
        out_shape=jax.ShapeDtypeStruct((N, D), F32),
        scratch_shapes=[pltpu.VMEM((tm, D), F32)],
        name="ffn",
        compiler_params=pltpu.CompilerParams(
            dimension_semantics=("parallel",), vmem_limit_bytes=VMEM_LIMIT),
    )(x1, p, wup, bup, wdn, bdn, wpe, wpg, g, beta)


def _pad_groups(w):
    z = jnp.zeros(w.shape[:-1] + (HEAD_DIM,), w.dtype)
    return jnp.concatenate([w[..., :HEAD_DIM], z, w[..., HEAD_DIM:], z], axis=-1)


def _overlap_matrix_t(n_cmp_padded, n_slc):
    n_cmp = n_cmp_padded - 1
    c0 = np.arange(n_cmp) * CMP_STRIDE
    c1 = c0 + CMP_LEN - 1
    s0 = np.arange(n_slc) * SLC_BLOCK
    s1 = s0 + SLC_BLOCK - 1
    m = ((c0[:, None] <= s1[None, :]) & (c1[:, None] >= s0[None, :])).astype(np.float32)
    out = np.zeros((n_slc, n_cmp_padded), np.float32)
    out[:, :n_cmp] = m.T
    return out


def _layer(x, p, w_in, b_in, conv_dw_w, conv_dw_b, conv_ln_g, conv_ln_b,
           cmp_pos_k, cmp_w1_k, cmp_w2_k, cmp_pos_v, cmp_w1_v, cmp_w2_v,
           w_out, b_out, ln1_g, ln1_b, w_up, b_up, w_down, b_down, w_pe, w_pg, ln2_g, ln2_b):
    B, S, D = x.shape
    scale = HEAD_DIM ** -0.5
    row = lambda v: v.reshape(1, -1).astype(F32)

    w, bias = w_in, b_in
    wrm = jnp.concatenate(
        [w[:, 0:1024], w[:, 1536:1792], _pad_groups(w[:, 1792:1920]), _pad_groups(w[:, 2048:2176])],
        axis=1).astype(BF16)
    brm = row(jnp.concatenate(
        [bias[0:1024], bias[1536:1792], _pad_groups(bias[1792:1920]), _pad_groups(bias[2048:2176])]))
    wt = jnp.concatenate(
        [w[:, 1024:1536] * scale, w[:, 1920:2048], w[:, 2176:2304], w[:, 2304:2328],
         jnp.zeros((D, 8), F32)], axis=1).T.astype(BF16)
    bt = jnp.concatenate(
        [bias[1024:1536] * scale, bias[1920:2048], bias[2176:2304], bias[2304:2328],
         jnp.zeros((8,), F32)]).reshape(-1, 1)

    vglu, kc, vc, ks, kw, qT, vsT, vwT, gT = _proj_call(x, wrm, brm, wt, bt)

    conv_w = jnp.concatenate([conv_dw_w.reshape(CONV_WIDTH, CONV_CH),
                              jnp.zeros((1, CONV_CH), F32)], axis=0)
    y_conv = _conv_call(vglu, conv_w, row(conv_dw_b), row(conv_ln_g), row(conv_ln_b))

    n_cmp_p = S // CMP_STRIDE
    blk_w = CMP_STRIDE * HEAD_DIM
    rk = kc.reshape(B, N_KV, n_cmp_p, blk_w)
    rv = vc.reshape(B, N_KV, n_cmp_p, blk_w)
    w2k = jnp.concatenate([cmp_w2_k, jnp.zeros((CMP_HIDDEN, 128 - HEAD_DIM), F32)], axis=1)
    kcz, vcT = _compress_call(
        rk, rv, cmp_pos_k.reshape(2, blk_w), cmp_w1_k.astype(BF16), w2k.astype(BF16),
        cmp_pos_v.reshape(2, blk_w), cmp_w1_v.astype(BF16), cmp_w2_v.T.astype(BF16))

    n_slc = S // SLC_BLOCK
    e_mat = jnp.asarray(
        (np.arange(S)[:, None] // SLC_BLOCK == np.arange(128)[None, :]).astype(np.float32), BF16)
    movT = jnp.asarray(_overlap_matrix_t(n_cmp_p, n_slc), BF16)
    y_nsaT = _nsa_call(qT, gT, kcz, vcT, ks, vsT, kw, vwT, e_mat, movT)

    wo = w_out.astype(BF16)
    x1 = _outproj_call(y_conv, y_nsaT, x, wo[:CONV_CH], wo[CONV_CH:], row(b_out),
                       row(ln1_g), row(ln1_b))

    out = _ffn_call(x1.reshape(B * S, D), p.reshape(B * S, PLE_DIM),
                    w_up.astype(BF16), row(b_up), w_down.astype(BF16), row(b_down),
                    w_pe.astype(BF16), w_pg.astype(BF16), row(ln2_g), row(ln2_b))
    return out.reshape(B, S, D)


def kernel(x, p, w_in, b_in, conv_dw_w, conv_dw_b, conv_ln_g, conv_ln_b, cmp_pos_k, cmp_w1_k, cmp_w2_k, cmp_pos_v, cmp_w1_v, cmp_w2_v, w_out, b_out, ln1_g, ln1_b, w_up, b_up, w_down, b_down, w_pe, w_pg, ln2_g, ln2_b):
    params = (w_in, b_in, conv_dw_w, conv_dw_b, conv_ln_g, conv_ln_b,
              cmp_pos_k, cmp_w1_k, cmp_w2_k, cmp_pos_v, cmp_w1_v, cmp_w2_v,
              w_out, b_out, ln1_g, ln1_b, w_up, b_up, w_down, b_down, w_pe, w_pg, ln2_g, ln2_b)
    for i in range(DEPTH):
        x = _layer(x, p[i], *[t[i] for t in params])
    return x
```

```python
import functools

import jax
import jax.numpy as jnp
import numpy as np
from jax import lax
from jax.experimental import pallas as pl
from jax.experimental.pallas import tpu as pltpu

F32 = jnp.float32
BF16 = jnp.bfloat16

D_MODEL = 1024
PLE_DIM = 256
CONV_CH = 512
CONV_WIDTH = 31
HEAD_DIM = 64
N_KV = 2
HPG = 4
N_BRANCH = 3
CMP_LEN = 32
CMP_STRIDE = 16
CMP_HIDDEN = 256
SLC_BLOCK = 64
SLC_TOPN = 16
WINDOW = 512
D_FF = 4 * D_MODEL
LN_EPS = 1e-5
NEG_INF = -1e30
FORCE_BONUS = 1e4
DEPTH = 1
ALPHA = (2 * DEPTH) ** 0.25

UNSELECTED_BIAS = -float(2 ** 30)
BELOW_NEG_INF = -3e38

PROJ_ROWS = 512
CONV_ROWS = 256
CONV_HALO = 32
CONV_ROW_CHUNK = 32
Q_TILE = 256
KEY_CHUNK = 256
WIN_UNIT = 128
OUT_ROWS = 512
FFN_ROWS = 512
FFN_CHUNK = 1024
VMEM_LIMIT = 56 * 1024 * 1024

RM_COLS = 1792
VT_GROUP_ROWS = 80
VT_ROWS = N_KV * VT_GROUP_ROWS
T_ROWS = 512 + 2 * VT_ROWS + 32


def _layer_norm(z, g, b):
    mu = jnp.mean(z, axis=-1, keepdims=True)
    zc = z - mu
    var = jnp.mean(zc * zc, axis=-1, keepdims=True)
    return zc * lax.rsqrt(var + LN_EPS) * g + b


def _dot(a, b):
    return jnp.dot(a, b, preferred_element_type=F32)


def _dot_nt(a, b):
    return lax.dot_general(a, b, (((1,), (1,)), ((), ())), preferred_element_type=F32)


def _dot_tn(a, b):
    return lax.dot_general(a, b, (((0,), (0,)), ((), ())), preferred_element_type=F32)


def _proj_kernel(x_ref, wrm_ref, brm_ref, wt_ref, bt_ref,
                 vglu_ref, kc_ref, vc_ref, ks_ref, kw_ref,
                 qT_ref, vsT_ref, vwT_ref, gT_ref):
    xb = x_ref[0].astype(BF16)

    def rm(c0, c1):
        return _dot(xb, wrm_ref[:, c0:c1]) + brm_ref[:, c0:c1]

    half = CONV_CH // 2
    for c in range(0, CONV_CH, half):
        a = rm(c, c + half)
        g = rm(CONV_CH + c, CONV_CH + c + half)
        vglu_ref[0, :, c:c + half] = a * jax.nn.sigmoid(g)

    kv = rm(1024, 1280)
    kc_ref[0, 0] = kv[:, 0:64]
    kc_ref[0, 1] = kv[:, 64:128]
    vc_ref[0, 0] = kv[:, 128:192]
    vc_ref[0, 1] = kv[:, 192:256]

    kz = rm(1280, 1792).astype(BF16)
    ks_ref[0, 0] = kz[:, 0:128]
    ks_ref[0, 1] = kz[:, 128:256]
    kw_ref[0, 0] = kz[:, 256:384]
    kw_ref[0, 1] = kz[:, 384:512]

    def tr(r0, r1):
        return _dot_nt(wt_ref[r0:r1, :], xb) + bt_ref[r0:r1, :]

    qT_ref[0] = tr(0, 512).astype(BF16)
    r = 512
    vs = tr(r, r + VT_ROWS).astype(BF16)
    for u in range(PROJ_ROWS // KEY_CHUNK):
        vsT_ref[0, u] = vs[:, u * KEY_CHUNK:(u + 1) * KEY_CHUNK]
    r += VT_ROWS
    vw = tr(r, r + VT_ROWS).astype(BF16)
    for u in range(PROJ_ROWS // WIN_UNIT):
        vwT_ref[0, u] = vw[:, u * WIN_UNIT:(u + 1) * WIN_UNIT]
    r += VT_ROWS
    gT_ref[0] = jax.nn.sigmoid(tr(r, r + 32))


def _proj_call(x, wrm, brm, wt, bt):
    B, S, D = x.shape
    tm = PROJ_ROWS
    n = S // tm
    const = lambda b, i: (0, 0)
    out_shape = (
        jax.ShapeDtypeStruct((B, S, CONV_CH), F32),
        jax.ShapeDtypeStruct((B, N_KV, S, HEAD_DIM), F32),
        jax.ShapeDtypeStruct((B, N_KV, S, HEAD_DIM), F32),
        jax.ShapeDtypeStruct((B, N_KV, S, 128), BF16),
        jax.ShapeDtypeStruct((B, N_KV, S, 128), BF16),
        jax.ShapeDtypeStruct((B, 512, S), BF16),
        jax.ShapeDtypeStruct((B, S // KEY_CHUNK, VT_ROWS, KEY_CHUNK), BF16),
        jax.ShapeDtypeStruct((B, S // WIN_UNIT, VT_ROWS, WIN_UNIT), BF16),
        jax.ShapeDtypeStruct((B, 32, S), F32),
    )
    kvspec = pl.BlockSpec((1, N_KV, tm, HEAD_DIM), lambda b, i: (b, 0, i, 0))
    kzspec = pl.BlockSpec((1, N_KV, tm, 128), lambda b, i: (b, 0, i, 0))
    out_specs = (
        pl.BlockSpec((1, tm, CONV_CH), lambda b, i: (b, i, 0)),
        kvspec, kvspec, kzspec, kzspec,
        pl.BlockSpec((1, 512, tm), lambda b, i: (b, 0, i)),
        pl.BlockSpec((1, tm // KEY_CHUNK, VT_ROWS, KEY_CHUNK), lambda b, i: (b, i, 0, 0)),
        pl.BlockSpec((1, tm // WIN_UNIT, VT_ROWS, WIN_UNIT), lambda b, i: (b, i, 0, 0)),
        pl.BlockSpec((1, 32, tm), lambda b, i: (b, 0, i)),
    )
    in_specs = [
        pl.BlockSpec((1, tm, D), lambda b, i: (b, i, 0)),
        pl.BlockSpec((D, RM_COLS), const),
        pl.BlockSpec((1, RM_COLS), const),
        pl.BlockSpec((T_ROWS, D), const),
        pl.BlockSpec((T_ROWS, 1), const),
    ]
    return pl.pallas_call(
        _proj_kernel, grid=(B, n), in_specs=in_specs, out_specs=out_specs,
        out_shape=out_shape, name="proj",
        compiler_params=pltpu.CompilerParams(
            dimension_semantics=("parallel", "parallel"), vmem_limit_bytes=VMEM_LIMIT),
    )(x, wrm, brm, wt, bt)


def _conv_kernel(cur_ref, halo_ref, w_ref, b_ref, g_ref, beta_ref, o_ref, xs_ref):
    i = pl.program_id(1)
    xs_ref[0:CONV_HALO, :] = jnp.where(i > 0, halo_ref[0], 0.0)
    xs_ref[CONV_HALO:, :] = cur_ref[0]
    lead = CONV_HALO - (CONV_WIDTH - 1)
    for r in range(0, CONV_ROWS, CONV_ROW_CHUNK):
        acc = jnp.zeros((CONV_ROW_CHUNK, CONV_CH), F32)
        for k in range(CONV_WIDTH):
            acc = acc + xs_ref[r + lead + k:r + lead + k + CONV_ROW_CHUNK, :] * w_ref[k:k + 1, :]
        y = _layer_norm(acc + b_ref[...], g_ref[...], beta_ref[...])
        o_ref[0, r:r + CONV_ROW_CHUNK, :] = (y * jax.nn.sigmoid(y)).astype(BF16)


def _conv_call(v, w, b, g, beta):
    B, S, C = v.shape
    tc = CONV_ROWS
    ratio = tc // CONV_HALO
    const = lambda b_, i: (0, 0)
    return pl.pallas_call(
        _conv_kernel, grid=(B, S // tc),
        in_specs=[
            pl.BlockSpec((1, tc, C), lambda b_, i: (b_, i, 0)),
            pl.BlockSpec((1, CONV_HALO, C), lambda b_, i: (b_, jnp.maximum(i * ratio - 1, 0), 0)),
            pl.BlockSpec((32, C), const),
            pl.BlockSpec((1, C), const), pl.BlockSpec((1, C), const), pl.BlockSpec((1, C), const),
        ],
        out_specs=pl.BlockSpec((1, tc, C), lambda b_, i: (b_, i, 0)),
        out_shape=jax.ShapeDtypeStruct((B, S, C), BF16),
        scratch_shapes=[pltpu.VMEM((tc + CONV_HALO, C), F32)],
        name="conv",
        compiler_params=pltpu.CompilerParams(
            dimension_semantics=("parallel", "parallel"), vmem_limit_bytes=VMEM_LIMIT),
    )(v, v, w, b, g, beta)


def _compress_hidden(r, pos_ref, w1_ref):
    half = CMP_STRIDE * HEAD_DIM
    n_rows = r.shape[0]
    a = _dot((r + pos_ref[0:1, :]).astype(BF16), w1_ref[0:half, :])
    b = _dot((r + pos_ref[1:2, :]).astype(BF16), w1_ref[half:2 * half, :])
    h = a + pltpu.roll(b, n_rows - 1, axis=0)
    return (h * jax.nn.sigmoid(h)).astype(BF16)


def _compress_kernel(rk_ref, rv_ref, pk_ref, w1k_ref, w2k_ref, pv_ref, w1v_ref, w2vT_ref,
                     kc_ref, vcT_ref):
    hk = _compress_hidden(rk_ref[0, 0], pk_ref, w1k_ref)
    kc_ref[0, 0] = _dot(hk, w2k_ref[...]).astype(BF16)
    hv = _compress_hidden(rv_ref[0, 0], pv_ref, w1v_ref)
    vcT_ref[0, 0] = _dot_nt(w2vT_ref[...], hv).astype(BF16)


def _compress_call(rk, rv, pk, w1k, w2k, pv, w1v, w2vT):
    B, G, NC, W = rk.shape
    const = lambda b, g: (0, 0)
    rspec = pl.BlockSpec((1, 1, NC, W), lambda b, g: (b, g, 0, 0))
    return pl.pallas_call(
        _compress_kernel, grid=(B, G),
        in_specs=[
            rspec, rspec,
            pl.BlockSpec((2, W), const), pl.BlockSpec((2 * W, CMP_HIDDEN), const),
            pl.BlockSpec((CMP_HIDDEN, 128), const),
            pl.BlockSpec((2, W), const), pl.BlockSpec((2 * W, CMP_HIDDEN), const),
            pl.BlockSpec((HEAD_DIM, CMP_HIDDEN), const),
        ],
        out_specs=(
            pl.BlockSpec((1, 1, NC, 128), lambda b, g: (b, g, 0, 0)),
            pl.BlockSpec((1, 1, HEAD_DIM, NC), lambda b, g: (b, g, 0, 0)),
        ),
        out_shape=(
            jax.ShapeDtypeStruct((B, G, NC, 128), BF16),
            jax.ShapeDtypeStruct((B, G, HEAD_DIM, NC), BF16),
        ),
        name="compress",
        compiler_params=pltpu.CompilerParams(
            dimension_semantics=("parallel", "parallel"), vmem_limit_bytes=VMEM_LIMIT),
    )(rk, rv, pk, w1k, w2k, pv, w1v, w2vT)


def _nsa_kernel(q_ref, g_ref, kc_ref, vcT_ref, ks_ref, vsT_ref, kw_ref, vwT_ref,
                e_ref, kdead_ref, mov_ref, cbias_ref, wbias_ref, tbias_ref,
                o_ref, q2_ref, s_a, s_b, p_a, p_b, acc_ref):
    T = Q_TILE
    C = KEY_CHUNK
    grp = pl.program_id(1)
    qb = pl.program_id(2)
    t0 = qb * T
    n_cmp = kc_ref.shape[2]
    n_slc = mov_ref.shape[0]
    n_chunks = vsT_ref.shape[1]
    heads = range(HPG)

    q4 = q_ref[0]
    spare = jnp.where(lax.broadcasted_iota(jnp.int32, (HEAD_DIM, T), 0) == 0,
                      UNSELECTED_BIAS, 0.0).astype(BF16)
    q1 = [jnp.concatenate([q4[h * HEAD_DIM:(h + 1) * HEAD_DIM, :], spare], axis=0) for h in heads]
    pos_t = t0 + lax.broadcasted_iota(jnp.int32, (1, T), 1)

    c_off = pl.multiple_of(n_cmp - qb * (T // CMP_STRIDE), T // CMP_STRIDE)
    q1_all = jnp.concatenate(q1, axis=1)

    def all_heads(a):
        return jnp.concatenate([a] * HPG, axis=1)

    def head_cols(a):
        return [a[:, h * T:(h + 1) * T] for h in heads]

    sc = _dot(kc_ref[0, 0], q1_all) + all_heads(cbias_ref[pl.ds(c_off, n_cmp), :])
    m_c = jnp.max(sc, axis=0, keepdims=True)
    p_c = jnp.exp(sc - m_c)
    l_c = jnp.sum(p_c, axis=0, keepdims=True)
    p_c = p_c * jnp.where(m_c > 0.5 * NEG_INF, 1.0 / l_c, 0.0)
    o_c = head_cols(_dot(vcT_ref[0, 0], p_c.astype(BF16)))
    psum = functools.reduce(lambda a, b: a + b, head_cols(p_c))

    w_keys = WINDOW + T
    start = pl.multiple_of(jnp.maximum(t0 - WINDOW, 0), WIN_UNIT)
    w_off = pl.multiple_of(WINDOW - (t0 - start), WIN_UNIT)
    u0 = start // WIN_UNIT
    v_win = jnp.concatenate([vwT_ref[0, u0 + u] for u in range(w_keys // WIN_UNIT)], axis=1)
    sw = (_dot(kw_ref[0, 0, pl.ds(start, w_keys), :], q1_all)
          + all_heads(wbias_ref[pl.ds(w_off, w_keys), :]))
    p_w = jnp.exp(sw - jnp.max(sw, axis=0, keepdims=True)).astype(BF16)
    ow = _dot(v_win, p_w)
    o_w = head_cols(ow[0:HEAD_DIM] * (1.0 / ow[HEAD_DIM:HEAD_DIM + 1]))

    mov = mov_ref[...]
    p_hi = psum.astype(BF16)
    rem = psum - p_hi.astype(F32)
    p_mid = rem.astype(BF16)
    p_lo = (rem - p_mid.astype(F32)).astype(BF16)
    imp = _dot(mov, p_hi) + _dot(mov, p_mid) + _dot(mov, p_lo)

    blk = lax.broadcasted_iota(jnp.int32, (n_slc, T), 0)
    blk_f = blk.astype(F32)
    cur = jnp.right_shift(pos_t, 6)
    forced = (blk == 0) | (blk == cur) | (blk == cur - 1)
    valid_b = (blk * SLC_BLOCK) <= pos_t
    work = jnp.where(valid_b, imp + jnp.where(forced, FORCE_BONUS, 0.0), NEG_INF)
    sel = jnp.zeros((n_slc, T), F32)
    for _ in range(SLC_TOPN):
        mx = jnp.max(work, axis=0, keepdims=True)
        first = jnp.min(jnp.where(work == mx, blk_f, float(n_slc)), axis=0, keepdims=True)
        pick = blk_f == first
        sel = jnp.where(pick, 1.0, sel)
        work = jnp.where(pick, BELOW_NEG_INF, work)
    unsel = jnp.where((sel > 0.0) & valid_b, 0.0, UNSELECTED_BIAS).astype(BF16)
    for h in heads:
        q2_ref[h] = jnp.concatenate([q1[h], unsel], axis=0)

    n_full = t0 // C
    uq = qb - n_full * (C // T)

    def key_operand(c, live):
        c = jnp.minimum(c, n_chunks - 1)
        off = pl.multiple_of(c * C, C)
        ks_c = ks_ref[0, 0, pl.ds(off, C), :]
        if live is not None:
            ks_c = jnp.where(live, ks_c, kdead_ref[...])
        return jnp.concatenate([ks_c, e_ref[pl.ds(off, C), :]], axis=1)

    def scores(h, k2, sbuf, bias=None):
        s = _dot(k2, q2_ref[h])
        if bias is not None:
            s = s + bias
        sbuf[h] = s
        return jnp.max(s, axis=0, keepdims=True)

    def weights(h, sbuf, pbuf, cm, m_old):
        m_new = jnp.maximum(m_old, cm)
        pbuf[h] = jnp.exp(sbuf[h] - m_new).astype(BF16)
        return m_new, jnp.exp(m_old - m_new)

    def add_values(h, vt, pbuf, rescale):
        acc_ref[h] = rescale * acc_ref[h] + _dot(vt, pbuf[h])

    k2_t = key_operand(n_full, None)
    k2_0 = key_operand(0, 0 < n_full)
    t_off = pl.multiple_of((C // T - 1 - uq) * T, T)
    m0, cm0 = [], []
    for h in heads:
        cm_t = scores(h, k2_t, s_a, tbias_ref[pl.ds(t_off, C), :])
        cm0.append(scores(h, k2_0, s_b))
        acc_ref[h] = jnp.zeros((VT_GROUP_ROWS, T), F32)
        m0.append(weights(h, s_a, p_a, cm_t, cm_t)[0])

    def pair_body(i, carry):
        m, scale_pend, c_pend, cm_b = carry
        c0 = 2 * i
        k2_a = key_operand(c0 + 1, c0 + 1 < n_full)
        k2_b = key_operand(c0 + 2, c0 + 2 < n_full)
        vt_pend = vsT_ref[0, c_pend]
        vt_0 = vsT_ref[0, c0]
        m1, scale0, cm_a = [], [], []
        for h in heads:
            add_values(h, vt_pend, p_a, scale_pend[h])
            cm_a.append(scores(h, k2_a, s_a))
            m_h, s_h = weights(h, s_b, p_b, cm_b[h], m[h])
            m1.append(m_h)
            scale0.append(s_h)
        m2, scale1, cm_b2 = [], [], []
        for h in heads:
            cm_b2.append(scores(h, k2_b, s_b))
            add_values(h, vt_0, p_b, scale0[h])
            m_h, s_h = weights(h, s_a, p_a, cm_a[h], m1[h])
            m2.append(m_h)
            scale1.append(s_h)
        return tuple(m2), tuple(scale1), jnp.minimum(c0 + 1, n_chunks - 1), tuple(cm_b2)

    ones = tuple(jnp.ones((1, T), F32) for _ in heads)
    _, scale_pend, c_pend, _ = lax.fori_loop(
        0, (n_full + 1) // 2, pair_body, (tuple(m0), ones, n_full, tuple(cm0)))

    vt_pend = vsT_ref[0, c_pend]
    for h in heads:
        add_values(h, vt_pend, p_a, scale_pend[h])
        acc = acc_ref[h]
        o_s = acc[0:HEAD_DIM] * (1.0 / acc[HEAD_DIM:HEAD_DIM + 1])
        g_row = grp * (HPG * N_BRANCH) + h * N_BRANCH
        gate = [g_ref[0, pl.ds(g_row + br, 1), :] for br in range(N_BRANCH)]
        out = gate[0] * o_c[h] + gate[1] * o_s + gate[2] * o_w[h]
        o_ref[0, h * HEAD_DIM:(h + 1) * HEAD_DIM, :] = out.astype(BF16)


def _nsa_call(qT, gT, kc, vcT, ks, vsT, kw, vwT, consts):
    B, _, S = qT.shape
    G = N_KV
    T = Q_TILE
    C = KEY_CHUNK
    n_cmp = kc.shape[2]
    rows = HPG * HEAD_DIM
    const = lambda b, g, i: (0, 0)
    e_mat, kdead, movT, cbias, wbias, tbias = consts
    full = lambda a: pl.BlockSpec(a.shape, const)
    in_specs = [
        pl.BlockSpec((1, rows, T), lambda b, g, i: (b, g, i)),
        pl.BlockSpec((1, 32, T), lambda b, g, i: (b, 0, i)),
        pl.BlockSpec((1, 1, n_cmp, 128), lambda b, g, i: (b, g, 0, 0)),
        pl.BlockSpec((1, 1, HEAD_DIM, n_cmp), lambda b, g, i: (b, g, 0, 0)),
        pl.BlockSpec((1, 1, S, 128), lambda b, g, i: (b, g, 0, 0)),
        pl.BlockSpec((1, S // C, VT_GROUP_ROWS, C), lambda b, g, i: (b, 0, g, 0)),
        pl.BlockSpec((1, 1, S, 128), lambda b, g, i: (b, g, 0, 0)),
        pl.BlockSpec((1, S // WIN_UNIT, VT_GROUP_ROWS, WIN_UNIT), lambda b, g, i: (b, 0, g, 0)),
        full(e_mat), full(kdead), full(movT), full(cbias), full(wbias), full(tbias),
    ]
    return pl.pallas_call(
        _nsa_kernel, grid=(B, G, S // T), in_specs=in_specs,
        out_specs=pl.BlockSpec((1, rows, T), lambda b, g, i: (b, g, i)),
        out_shape=jax.ShapeDtypeStruct((B, G * rows, S), BF16),
        scratch_shapes=[pltpu.VMEM((HPG, 256, T), BF16),
                        pltpu.VMEM((HPG, C, T), F32), pltpu.VMEM((HPG, C, T), F32),
                        pltpu.VMEM((HPG, C, T), BF16), pltpu.VMEM((HPG, C, T), BF16),
                        pltpu.VMEM((HPG, VT_GROUP_ROWS, T), F32)],
        name="nsa",
        compiler_params=pltpu.CompilerParams(
            dimension_semantics=("parallel", "parallel", "arbitrary"),
            vmem_limit_bytes=VMEM_LIMIT),
    )(qT, gT, kc, vcT, ks, vsT, kw, vwT, e_mat, kdead, movT, cbias, wbias, tbias)


def _outproj_kernel(yc_ref, ynT_ref, x_ref, wc_ref, wn_ref, b_ref, g_ref, beta_ref, o_ref):
    mix = _dot(yc_ref[0], wc_ref[...]) + _dot_tn(ynT_ref[0], wn_ref[...])
    z = ALPHA * x_ref[0] + mix + b_ref[...]
    o_ref[0] = _layer_norm(z, g_ref[...], beta_ref[...])


def _outproj_call(yc, ynT, x, wc, wn, b, g, beta):
    B, S, D = x.shape
    tm = OUT_ROWS
    const = lambda b_, i: (0, 0)
    vec = pl.BlockSpec((1, D), const)
    return pl.pallas_call(
        _outproj_kernel, grid=(B, S // tm),
        in_specs=[
            pl.BlockSpec((1, tm, CONV_CH), lambda b_, i: (b_, i, 0)),
            pl.BlockSpec((1, D - CONV_CH, tm), lambda b_, i: (b_, 0, i)),
            pl.BlockSpec((1, tm, D), lambda b_, i: (b_, i, 0)),
            pl.BlockSpec((CONV_CH, D), const), pl.BlockSpec((D - CONV_CH, D), const),
            vec, vec, vec,
        ],
        out_specs=pl.BlockSpec((1, tm, D), lambda b_, i: (b_, i, 0)),
        out_shape=jax.ShapeDtypeStruct((B, S, D), F32),
        name="outproj",
        compiler_params=pltpu.CompilerParams(
            dimension_semantics=("parallel", "parallel"), vmem_limit_bytes=VMEM_LIMIT),
    )(yc, ynT, x, wc, wn, b, g, beta)


def _ffn_kernel(x_ref, p_ref, wup_ref, bup_ref, wdn_ref, bdn_ref, wpe_ref, wpg_ref,
                g_ref, beta_ref, o_ref, acc_ref):
    x1 = x_ref[...]
    xb = x1.astype(BF16)
    ple = _dot(p_ref[...].astype(BF16), wpe_ref[...]) * jax.nn.sigmoid(_dot(xb, wpg_ref[...]))
    acc_ref[...] = ALPHA * x1 + ple + bdn_ref[...]
    for c in range(0, D_FF, FFN_CHUNK):
        u = _dot(xb, wup_ref[:, c:c + FFN_CHUNK]) + bup_ref[:, c:c + FFN_CHUNK]
        u = jnp.square(jnp.maximum(u, 0.0)).astype(BF16)
        acc_ref[...] += _dot(u, wdn_ref[c:c + FFN_CHUNK, :])
    o_ref[...] = _layer_norm(acc_ref[...], g_ref[...], beta_ref[...])


def _ffn_call(x1, p, wup, bup, wdn, bdn, wpe, wpg, g, beta):
    N, D = x1.shape
    tm = FFN_ROWS
    const = lambda i: (0, 0)
    single = pl.Buffered(1)
    vec = pl.BlockSpec((1, D), const)
    return pl.pallas_call(
        _ffn_kernel, grid=(N // tm,),
        in_specs=[
            pl.BlockSpec((tm, D), lambda i: (i, 0)),
            pl.BlockSpec((tm, PLE_DIM), lambda i: (i, 0)),
            pl.BlockSpec((D, D_FF), const, pipeline_mode=single),
            pl.BlockSpec((1, D_FF), const),
            pl.BlockSpec((D_FF, D), const, pipeline_mode=single),
            vec,
            pl.BlockSpec((PLE_DIM, D), const, pipeline_mode=single),
            pl.BlockSpec((D, D), const, pipeline_mode=single),
            vec, vec,
        ],
        out_specs=pl.BlockSpec((tm, D), lambda i: (i, 0)),
        out_shape=jax.ShapeDtypeStruct((N, D), F32),
        scratch_shapes=[pltpu.VMEM((tm, D), F32)],
        name="ffn",
        compiler_params=pltpu.CompilerParams(
            dimension_semantics=("parallel",), vmem_limit_bytes=VMEM_LIMIT),
    )(x1, p, wup, bup, wdn, bdn, wpe, wpg, g, beta)


def _pad_groups(w):
    z = jnp.zeros(w.shape[:-1] + (HEAD_DIM,), w.dtype)
    return jnp.concatenate([w[..., :HEAD_DIM], z, w[..., HEAD_DIM:], z], axis=-1)


def _overlap_matrix_t(n_cmp_padded, n_slc):
    n_cmp = n_cmp_padded - 1
    c0 = np.arange(n_cmp) * CMP_STRIDE
    c1 = c0 + CMP_LEN - 1
    s0 = np.arange(n_slc) * SLC_BLOCK
    s1 = s0 + SLC_BLOCK - 1
    m = ((c0[:, None] <= s1[None, :]) & (c1[:, None] >= s0[None, :])).astype(np.float32)
    out = np.zeros((n_slc, n_cmp_padded), np.float32)
    out[:, :n_cmp] = m.T
    return out


def _vt_rows(w, bias):
    d = w.shape[0]
    pad = VT_GROUP_ROWS - HEAD_DIM
    one = jnp.zeros((pad,), F32).at[0].set(1.0)
    ws, bs = [], []
    for g in range(N_KV):
        ws += [w[:, g * HEAD_DIM:(g + 1) * HEAD_DIM], jnp.zeros((d, pad), F32)]
        bs += [bias[g * HEAD_DIM:(g + 1) * HEAD_DIM], one]
    return jnp.concatenate(ws, axis=1), jnp.concatenate(bs)


def _nsa_constants(S, n_cmp_p):
    T, C = Q_TILE, KEY_CHUNK
    t = np.arange(T)[None, :]
    neg = np.float32(NEG_INF)

    def bias(valid):
        return jnp.asarray(np.where(valid, np.float32(0.0), neg))

    e_mat = (np.arange(S)[:, None] // SLC_BLOCK == np.arange(S // SLC_BLOCK)[None, :])
    kdead = np.zeros((C, 128), np.float32)
    kdead[:, HEAD_DIM] = 1.0
    j = np.arange(2 * n_cmp_p)[:, None]
    cbias = bias(CMP_STRIDE * (j - n_cmp_p) + CMP_LEN - 1 <= t)
    j = np.arange(2 * WINDOW + T)[:, None]
    wbias = bias((t < j) & (j <= WINDOW + t))
    j = np.arange(2 * C - T)[:, None]
    tbias = bias(j - (C - T) <= t)
    return (jnp.asarray(e_mat.astype(np.float32), BF16), jnp.asarray(kdead, BF16),
            jnp.asarray(_overlap_matrix_t(n_cmp_p, S // SLC_BLOCK), BF16), cbias, wbias, tbias)


def _layer(x, p, w_in, b_in, conv_dw_w, conv_dw_b, conv_ln_g, conv_ln_b,
           cmp_pos_k, cmp_w1_k, cmp_w2_k, cmp_pos_v, cmp_w1_v, cmp_w2_v,
           w_out, b_out, ln1_g, ln1_b, w_up, b_up, w_down, b_down, w_pe, w_pg, ln2_g, ln2_b):
    B, S, D = x.shape
    scale = HEAD_DIM ** -0.5
    row = lambda v: v.reshape(1, -1).astype(F32)

    w, bias = w_in, b_in
    wrm = jnp.concatenate(
        [w[:, 0:1024], w[:, 1536:1792], _pad_groups(w[:, 1792:1920]), _pad_groups(w[:, 2048:2176])],
        axis=1).astype(BF16)
    brm = row(jnp.concatenate(
        [bias[0:1024], bias[1536:1792], _pad_groups(bias[1792:1920]), _pad_groups(bias[2048:2176])]))
    w_vs, b_vs = _vt_rows(w[:, 1920:2048], bias[1920:2048])
    w_vw, b_vw = _vt_rows(w[:, 2176:2304], bias[2176:2304])
    wt = jnp.concatenate(
        [w[:, 1024:1536] * scale, w_vs, w_vw, w[:, 2304:2328], jnp.zeros((D, 8), F32)],
        axis=1).T.astype(BF16)
    bt = jnp.concatenate(
        [bias[1024:1536] * scale, b_vs, b_vw, bias[2304:2328], jnp.zeros((8,), F32)]).reshape(-1, 1)

    vglu, kc, vc, ks, kw, qT, vsT, vwT, gT = _proj_call(x, wrm, brm, wt, bt)

    conv_w = jnp.concatenate([conv_dw_w.reshape(CONV_WIDTH, CONV_CH),
                              jnp.zeros((1, CONV_CH), F32)], axis=0)
    y_conv = _conv_call(vglu, conv_w, row(conv_dw_b), row(conv_ln_g), row(conv_ln_b))

    n_cmp_p = S // CMP_STRIDE
    blk_w = CMP_STRIDE * HEAD_DIM
    rk = kc.reshape(B, N_KV, n_cmp_p, blk_w)
    rv = vc.reshape(B, N_KV, n_cmp_p, blk_w)
    w2k = jnp.concatenate([cmp_w2_k, jnp.zeros((CMP_HIDDEN, 128 - HEAD_DIM), F32)], axis=1)
    kcz, vcT = _compress_call(
        rk, rv, cmp_pos_k.reshape(2, blk_w), cmp_w1_k.astype(BF16), w2k.astype(BF16),
        cmp_pos_v.reshape(2, blk_w), cmp_w1_v.astype(BF16), cmp_w2_v.T.astype(BF16))

    y_nsaT = _nsa_call(qT, gT, kcz, vcT, ks, vsT, kw, vwT, _nsa_constants(S, n_cmp_p))

    wo = w_out.astype(BF16)
    x1 = _outproj_call(y_conv, y_nsaT, x, wo[:CONV_CH], wo[CONV_CH:], row(b_out),
                       row(ln1_g), row(ln1_b))

    out = _ffn_call(x1.reshape(B * S, D), p.reshape(B * S, PLE_DIM),
                    w_up.astype(BF16), row(b_up), w_down.astype(BF16), row(b_down),
                    w_pe.astype(BF16), w_pg.astype(BF16), row(ln2_g), row(ln2_b))
    return out.reshape(B, S, D)


def kernel(x, p, w_in, b_in, conv_dw_w, conv_dw_b, conv_ln_g, conv_ln_b, cmp_pos_k, cmp_w1_k, cmp_w2_k, cmp_pos_v, cmp_w1_v, cmp_w2_v, w_out, b_out, ln1_g, ln1_b, w_up, b_up, w_down, b_down, w_pe, w_pg, ln2_g, ln2_b):
    params = (w_in, b_in, conv_dw_w, conv_dw_b, conv_ln_g, conv_ln_b,
              cmp_pos_k, cmp_w1_k, cmp_w2_k, cmp_pos_v, cmp_w1_v, cmp_w2_v,
              w_out, b_out, ln1_g, ln1_b, w_up, b_up, w_down, b_down, w_pe, w_pg, ln2_g, ln2_b)
    for i in range(DEPTH):
        x = _layer(x, p[i], *[t[i] for t in params])
    return x
```

```python
import functools

import jax
import jax.numpy as jnp
import numpy as np
from jax import lax
from jax.experimental import pallas as pl
from jax.experimental.pallas import tpu as pltpu

F32 = jnp.float32
BF16 = jnp.bfloat16

D_MODEL = 1024
PLE_DIM = 256
CONV_CH = 512
CONV_WIDTH = 31
HEAD_DIM = 64
N_KV = 2
HPG = 4
N_BRANCH = 3
CMP_LEN = 32
CMP_STRIDE = 16
CMP_HIDDEN = 256
SLC_BLOCK = 64
SLC_TOPN = 16
WINDOW = 512
D_FF = 4 * D_MODEL
LN_EPS = 1e-5
NEG_INF = -1e30
FORCE_BONUS = 1e4
DEPTH = 1
ALPHA = (2 * DEPTH) ** 0.25

UNSELECTED_BIAS = -float(2 ** 30)
BELOW_NEG_INF = -3e38

SUBLANES = 8
PROJ_ROWS = 512
CONV_ROWS = 256
CONV_HALO = 32
CONV_ROW_CHUNK = 32
Q_TILE = 256
KEY_CHUNK = 256
WIN_UNIT = 128
CMP_UNIT = 128
LOG2_E = 1.4426950408889634
OUT_ROWS = 512
FFN_ROWS = 512
FFN_CHUNK = 1024
VMEM_LIMIT = 56 * 1024 * 1024

RM_COLS = 1792
VT_GROUP_ROWS = 80
VT_ROWS = N_KV * VT_GROUP_ROWS
T_ROWS = 512 + 2 * VT_ROWS + 32


def _layer_norm(z, g, b):
    mu = jnp.mean(z, axis=-1, keepdims=True)
    zc = z - mu
    var = jnp.mean(zc * zc, axis=-1, keepdims=True)
    return zc * lax.rsqrt(var + LN_EPS) * g + b


def _dot(a, b):
    return jnp.dot(a, b, preferred_element_type=F32)


def _dot_nt(a, b):
    return lax.dot_general(a, b, (((1,), (1,)), ((), ())), preferred_element_type=F32)


def _dot_tn(a, b):
    return lax.dot_general(a, b, (((0,), (0,)), ((), ())), preferred_element_type=F32)


def _proj_kernel(x_ref, wrm_ref, brm_ref, wt_ref, bt_ref,
                 vglu_ref, kc_ref, vc_ref, ks_ref, kw_ref,
                 qT_ref, vsT_ref, vwT_ref, gT_ref):
    xb = x_ref[0].astype(BF16)

    def rm(c0, c1):
        return _dot(xb, wrm_ref[:, c0:c1]) + brm_ref[:, c0:c1]

    half = CONV_CH // 2
    for c in range(0, CONV_CH, half):
        a = rm(c, c + half)
        g = rm(CONV_CH + c, CONV_CH + c + half)
        vglu_ref[0, :, c:c + half] = a * jax.nn.sigmoid(g)

    kv = rm(1024, 1280)
    kc_ref[0, 0] = kv[:, 0:64]
    kc_ref[0, 1] = kv[:, 64:128]
    vc_ref[0, 0] = kv[:, 128:192]
    vc_ref[0, 1] = kv[:, 192:256]

    kz = rm(1280, 1792).astype(BF16)
    ks_ref[0, 0] = kz[:, 0:128]
    ks_ref[0, 1] = kz[:, 128:256]
    kw_ref[0, 0] = kz[:, 256:384]
    kw_ref[0, 1] = kz[:, 384:512]

    def tr(r0, r1):
        return _dot_nt(wt_ref[r0:r1, :], xb) + bt_ref[r0:r1, :]

    qT_ref[0] = tr(0, 512).astype(BF16)
    r = 512
    vs = tr(r, r + VT_ROWS).astype(BF16)
    for u in range(PROJ_ROWS // KEY_CHUNK):
        vsT_ref[0, u] = vs[:, u * KEY_CHUNK:(u + 1) * KEY_CHUNK]
    r += VT_ROWS
    vw = tr(r, r + VT_ROWS).astype(BF16)
    for u in range(PROJ_ROWS // WIN_UNIT):
        vwT_ref[0, u] = vw[:, u * WIN_UNIT:(u + 1) * WIN_UNIT]
    r += VT_ROWS
    gT_ref[0] = jax.nn.sigmoid(tr(r, r + 32))


def _proj_call(x, wrm, brm, wt, bt):
    B, S, D = x.shape
    tm = PROJ_ROWS
    n = S // tm
    const = lambda b, i: (0, 0)
    out_shape = (
        jax.ShapeDtypeStruct((B, S, CONV_CH), F32),
        jax.ShapeDtypeStruct((B, N_KV, S, HEAD_DIM), F32),
        jax.ShapeDtypeStruct((B, N_KV, S, HEAD_DIM), F32),
        jax.ShapeDtypeStruct((B, N_KV, S, 128), BF16),
        jax.ShapeDtypeStruct((B, N_KV, S, 128), BF16),
        jax.ShapeDtypeStruct((B, 512, S), BF16),
        jax.ShapeDtypeStruct((B, S // KEY_CHUNK, VT_ROWS, KEY_CHUNK), BF16),
        jax.ShapeDtypeStruct((B, S // WIN_UNIT, VT_ROWS, WIN_UNIT), BF16),
        jax.ShapeDtypeStruct((B, 32, S), F32),
    )
    kvspec = pl.BlockSpec((1, N_KV, tm, HEAD_DIM), lambda b, i: (b, 0, i, 0))
    kzspec = pl.BlockSpec((1, N_KV, tm, 128), lambda b, i: (b, 0, i, 0))
    out_specs = (
        pl.BlockSpec((1, tm, CONV_CH), lambda b, i: (b, i, 0)),
        kvspec, kvspec, kzspec, kzspec,
        pl.BlockSpec((1, 512, tm), lambda b, i: (b, 0, i)),
        pl.BlockSpec((1, tm // KEY_CHUNK, VT_ROWS, KEY_CHUNK), lambda b, i: (b, i, 0, 0)),
        pl.BlockSpec((1, tm // WIN_UNIT, VT_ROWS, WIN_UNIT), lambda b, i: (b, i, 0, 0)),
        pl.BlockSpec((1, 32, tm), lambda b, i: (b, 0, i)),
    )
    in_specs = [
        pl.BlockSpec((1, tm, D), lambda b, i: (b, i, 0)),
        pl.BlockSpec((D, RM_COLS), const),
        pl.BlockSpec((1, RM_COLS), const),
        pl.BlockSpec((T_ROWS, D), const),
        pl.BlockSpec((T_ROWS, 1), const),
    ]
    return pl.pallas_call(
        _proj_kernel, grid=(B, n), in_specs=in_specs, out_specs=out_specs,
        out_shape=out_shape, name="proj",
        compiler_params=pltpu.CompilerParams(
            dimension_semantics=("parallel", "parallel"), vmem_limit_bytes=VMEM_LIMIT),
    )(x, wrm, brm, wt, bt)


def _conv_kernel(cur_ref, halo_ref, w_ref, b_ref, g_ref, beta_ref, o_ref, xs_ref):
    i = pl.program_id(1)
    n_shift = xs_ref.shape[1]
    xs_ref[0, 0:CONV_HALO, :] = jnp.where(i > 0, halo_ref[0], 0.0)
    xs_ref[0, CONV_HALO:, :] = cur_ref[0]
    x0 = xs_ref[0]
    for b in range(1, SUBLANES):
        xs_ref[b] = pltpu.roll(x0, n_shift - b, axis=0)
    lead = CONV_HALO - (CONV_WIDTH - 1)
    for r in range(0, CONV_ROWS, CONV_ROW_CHUNK):
        acc = jnp.zeros((CONV_ROW_CHUNK, CONV_CH), F32)
        for k in range(CONV_WIDTH):
            a, b = divmod(lead + k, SUBLANES)
            row = r + a * SUBLANES
            w_k = jnp.concatenate([w_ref[k]] * (CONV_ROW_CHUNK // SUBLANES), axis=0)
            acc = acc + xs_ref[b, row:row + CONV_ROW_CHUNK, :] * w_k
        y = _layer_norm(acc + b_ref[...], g_ref[...], beta_ref[...])
        o_ref[0, r:r + CONV_ROW_CHUNK, :] = (y * jax.nn.sigmoid(y)).astype(BF16)


def _conv_call(v, w, b, g, beta):
    B, S, C = v.shape
    tc = CONV_ROWS
    ratio = tc // CONV_HALO
    const = lambda b_, i: (0, 0)
    return pl.pallas_call(
        _conv_kernel, grid=(B, S // tc),
        in_specs=[
            pl.BlockSpec((1, tc, C), lambda b_, i: (b_, i, 0)),
            pl.BlockSpec((1, CONV_HALO, C), lambda b_, i: (b_, jnp.maximum(i * ratio - 1, 0), 0)),
            pl.BlockSpec((CONV_WIDTH, SUBLANES, C), lambda b_, i: (0, 0, 0)),
            pl.BlockSpec((1, C), const), pl.BlockSpec((1, C), const), pl.BlockSpec((1, C), const),
        ],
        out_specs=pl.BlockSpec((1, tc, C), lambda b_, i: (b_, i, 0)),
        out_shape=jax.ShapeDtypeStruct((B, S, C), BF16),
        scratch_shapes=[pltpu.VMEM((SUBLANES, tc + CONV_HALO, C), F32)],
        name="conv",
        compiler_params=pltpu.CompilerParams(
            dimension_semantics=("parallel", "parallel"), vmem_limit_bytes=VMEM_LIMIT),
    )(v, v, w, b, g, beta)


def _compress_hidden(r, pos_ref, w1_ref):
    half = CMP_STRIDE * HEAD_DIM
    n_rows = r.shape[0]
    a = _dot((r + pos_ref[0:1, :]).astype(BF16), w1_ref[0:half, :])
    b = _dot((r + pos_ref[1:2, :]).astype(BF16), w1_ref[half:2 * half, :])
    h = a + pltpu.roll(b, n_rows - 1, axis=0)
    return (h * jax.nn.sigmoid(h)).astype(BF16)


def _compress_kernel(rk_ref, rv_ref, pk_ref, w1k_ref, w2k_ref, pv_ref, w1v_ref, w2vT_ref,
                     kc_ref, vcT_ref):
    hk = _compress_hidden(rk_ref[0, 0], pk_ref, w1k_ref)
    kc_ref[0, 0] = _dot(hk, w2k_ref[...]).astype(BF16)
    hv = _compress_hidden(rv_ref[0, 0], pv_ref, w1v_ref)
    vcT_ref[0, 0] = _dot_nt(w2vT_ref[...], hv).astype(BF16)


def _compress_call(rk, rv, pk, w1k, w2k, pv, w1v, w2vT):
    B, G, NC, W = rk.shape
    const = lambda b, g: (0, 0)
    rspec = pl.BlockSpec((1, 1, NC, W), lambda b, g: (b, g, 0, 0))
    return pl.pallas_call(
        _compress_kernel, grid=(B, G),
        in_specs=[
            rspec, rspec,
            pl.BlockSpec((2, W), const), pl.BlockSpec((2 * W, CMP_HIDDEN), const),
            pl.BlockSpec((CMP_HIDDEN, 128), const),
            pl.BlockSpec((2, W), const), pl.BlockSpec((2 * W, CMP_HIDDEN), const),
            pl.BlockSpec((HEAD_DIM, CMP_HIDDEN), const),
        ],
        out_specs=(
            pl.BlockSpec((1, 1, NC, 128), lambda b, g: (b, g, 0, 0)),
            pl.BlockSpec((1, 1, HEAD_DIM, NC), lambda b, g: (b, g, 0, 0)),
        ),
        out_shape=(
            jax.ShapeDtypeStruct((B, G, NC, 128), BF16),
            jax.ShapeDtypeStruct((B, G, HEAD_DIM, NC), BF16),
        ),
        name="compress",
        compiler_params=pltpu.CompilerParams(
            dimension_semantics=("parallel", "parallel"), vmem_limit_bytes=VMEM_LIMIT),
    )(rk, rv, pk, w1k, w2k, pv, w1v, w2vT)


def _nsa_kernel(q_ref, g_ref, kc_ref, vcT_ref, ks_ref, vsT_ref, kw_ref, vwT_ref,
                e_ref, kdead_ref, mov_ref, cbias_ref, wbias_ref, tbias_ref,
                o_ref, q2_ref, s_a, s_b, p_a, p_b, acc_ref, oc_ref, imp_ref):
    T = Q_TILE
    C = KEY_CHUNK
    grp = pl.program_id(1)
    qb = pl.program_id(2)
    t0 = qb * T
    n_cmp = kc_ref.shape[2]
    n_slc = mov_ref.shape[0]
    n_chunks = vsT_ref.shape[1]
    heads = range(HPG)

    q4 = q_ref[0]
    spare = jnp.where(lax.broadcasted_iota(jnp.int32, (HEAD_DIM, T), 0) == 0,
                      UNSELECTED_BIAS, 0.0).astype(BF16)
    q1 = [jnp.concatenate([q4[h * HEAD_DIM:(h + 1) * HEAD_DIM, :], spare], axis=0) for h in heads]
    pos_t = t0 + lax.broadcasted_iota(jnp.int32, (1, T), 1)

    c_off = pl.multiple_of(n_cmp - qb * (T // CMP_STRIDE), T // CMP_STRIDE)
    q1_all = jnp.concatenate(q1, axis=1)

    def all_heads(a):
        return jnp.concatenate([a] * HPG, axis=1)

    def head_cols(a):
        return [a[:, h * T:(h + 1) * T] for h in heads]

    def compressed(rows):
        sc = (_dot(kc_ref[0, 0, 0:rows, :], q1_all)
              + all_heads(cbias_ref[pl.ds(c_off, rows), :]))
        m_c = jnp.max(sc, axis=0, keepdims=True)
        p_c = jnp.exp2(sc - m_c)
        l_c = jnp.sum(p_c, axis=0, keepdims=True)
        p_c = p_c * jnp.where(m_c > 0.5 * NEG_INF, 1.0 / l_c, 0.0)
        oc_ref[...] = _dot(vcT_ref[0, 0, :, 0:rows], p_c.astype(BF16))
        psum = functools.reduce(lambda a, b: a + b, head_cols(p_c))
        mov = mov_ref[:, 0:rows]
        p_hi = psum.astype(BF16)
        rem = psum - p_hi.astype(F32)
        p_mid = rem.astype(BF16)
        p_lo = (rem - p_mid.astype(F32)).astype(BF16)
        imp_ref[...] = _dot(mov, p_hi) + _dot(mov, p_mid) + _dot(mov, p_lo)

    cmp_units = (t0 + T - CMP_LEN) // CMP_STRIDE // CMP_UNIT + 1
    for units in range(1, n_cmp // CMP_UNIT + 1):
        pl.when(cmp_units == units)(functools.partial(compressed, units * CMP_UNIT))
    o_c = head_cols(oc_ref[...])
    imp = imp_ref[...]

    w_keys = WINDOW + T
    start = pl.multiple_of(jnp.maximum(t0 - WINDOW, 0), WIN_UNIT)
    w_off = pl.multiple_of(WINDOW - (t0 - start), WIN_UNIT)
    u0 = start // WIN_UNIT
    v_win = jnp.concatenate([vwT_ref[0, u0 + u] for u in range(w_keys // WIN_UNIT)], axis=1)
    sw = (_dot(kw_ref[0, 0, pl.ds(start, w_keys), :], q1_all)
          + all_heads(wbias_ref[pl.ds(w_off, w_keys), :]))
    p_w = jnp.exp2(sw - jnp.max(sw, axis=0, keepdims=True)).astype(BF16)
    ow = _dot(v_win, p_w)
    o_w = head_cols(ow[0:HEAD_DIM] * (1.0 / ow[HEAD_DIM:HEAD_DIM + 1]))

    blk = lax.broadcasted_iota(jnp.int32, (n_slc, T), 0)
    blk_f = blk.astype(F32)
    cur = jnp.right_shift(pos_t, 6)
    forced = (blk == 0) | (blk == cur) | (blk == cur - 1)
    valid_b = (blk * SLC_BLOCK) <= pos_t
    work = jnp.where(valid_b, imp + jnp.where(forced, FORCE_BONUS, 0.0), NEG_INF)
    for _ in range(SLC_TOPN):
        mx = jnp.max(work, axis=0, keepdims=True)
        first = jnp.min(jnp.where(work == mx, blk_f, float(n_slc)), axis=0, keepdims=True)
        work = jnp.where(blk_f == first, BELOW_NEG_INF, work)
    picked = work < 0.5 * BELOW_NEG_INF
    unsel = jnp.where(picked & valid_b, 0.0, UNSELECTED_BIAS).astype(BF16)
    for h in heads:
        q2_ref[h] = jnp.concatenate([q1[h], unsel], axis=0)

    n_full = t0 // C
    uq = qb - n_full * (C // T)

    def key_operand(c, live):
        c = jnp.minimum(c, n_chunks - 1)
        off = pl.multiple_of(c * C, C)
        ks_c = ks_ref[0, 0, pl.ds(off, C), :]
        if live is not None:
            ks_c = jnp.where(live, ks_c, kdead_ref[...])
        return jnp.concatenate([ks_c, e_ref[pl.ds(off, C), :]], axis=1)

    def scores(h, k2, sbuf, bias=None):
        s = _dot(k2, q2_ref[h])
        if bias is not None:
            s = s + bias
        sbuf[h] = s
        return jnp.max(s, axis=0, keepdims=True)

    def weights(h, sbuf, pbuf, cm, m_old):
        m_new = jnp.maximum(m_old, cm)
        pbuf[h] = jnp.exp2(sbuf[h] - m_new).astype(BF16)
        return m_new, jnp.exp2(m_old - m_new)

    def add_values(h, vt, pbuf, rescale):
        acc_ref[h] = rescale * acc_ref[h] + _dot(vt, pbuf[h])

    k2_t = key_operand(n_full, None)
    k2_0 = key_operand(0, 0 < n_full)
    t_off = pl.multiple_of((C // T - 1 - uq) * T, T)
    m0, cm0 = [], []
    for h in heads:
        cm_t = scores(h, k2_t, s_a, tbias_ref[pl.ds(t_off, C), :])
        cm0.append(scores(h, k2_0, s_b))
        acc_ref[h] = jnp.zeros((VT_GROUP_ROWS, T), F32)
        m0.append(weights(h, s_a, p_a, cm_t, cm_t)[0])

    def pair_body(i, carry):
        m, scale_pend, c_pend, cm_b = carry
        c0 = 2 * i
        k2_a = key_operand(c0 + 1, c0 + 1 < n_full)
        k2_b = key_operand(c0 + 2, c0 + 2 < n_full)
        vt_pend = vsT_ref[0, c_pend]
        vt_0 = vsT_ref[0, c0]
        m1, scale0, cm_a = [], [], []
        for h in heads:
            add_values(h, vt_pend, p_a, scale_pend[h])
            cm_a.append(scores(h, k2_a, s_a))
            m_h, s_h = weights(h, s_b, p_b, cm_b[h], m[h])
            m1.append(m_h)
            scale0.append(s_h)
        m2, scale1, cm_b2 = [], [], []
        for h in heads:
            cm_b2.append(scores(h, k2_b, s_b))
            add_values(h, vt_0, p_b, scale0[h])
            m_h, s_h = weights(h, s_a, p_a, cm_a[h], m1[h])
            m2.append(m_h)
            scale1.append(s_h)
        return tuple(m2), tuple(scale1), jnp.minimum(c0 + 1, n_chunks - 1), tuple(cm_b2)

    ones = tuple(jnp.ones((1, T), F32) for _ in heads)
    _, scale_pend, c_pend, _ = lax.fori_loop(
        0, (n_full + 1) // 2, pair_body, (tuple(m0), ones, n_full, tuple(cm0)))

    vt_pend = vsT_ref[0, c_pend]
    for h in heads:
        add_values(h, vt_pend, p_a, scale_pend[h])
        acc = acc_ref[h]
        o_s = acc[0:HEAD_DIM] * (1.0 / acc[HEAD_DIM:HEAD_DIM + 1])
        g_row = grp * (HPG * N_BRANCH) + h * N_BRANCH
        gate = [g_ref[0, pl.ds(g_row + br, 1), :] for br in range(N_BRANCH)]
        out = gate[0] * o_c[h] + gate[1] * o_s + gate[2] * o_w[h]
        o_ref[0, h * HEAD_DIM:(h + 1) * HEAD_DIM, :] = out.astype(BF16)


def _nsa_call(qT, gT, kc, vcT, ks, vsT, kw, vwT, consts):
    B, _, S = qT.shape
    G = N_KV
    T = Q_TILE
    C = KEY_CHUNK
    n_cmp = kc.shape[2]
    rows = HPG * HEAD_DIM
    const = lambda b, g, i: (0, 0)
    e_mat, kdead, movT, cbias, wbias, tbias = consts
    full = lambda a: pl.BlockSpec(a.shape, const)
    in_specs = [
        pl.BlockSpec((1, rows, T), lambda b, g, i: (b, g, i)),
        pl.BlockSpec((1, 32, T), lambda b, g, i: (b, 0, i)),
        pl.BlockSpec((1, 1, n_cmp, 128), lambda b, g, i: (b, g, 0, 0)),
        pl.BlockSpec((1, 1, HEAD_DIM, n_cmp), lambda b, g, i: (b, g, 0, 0)),
        pl.BlockSpec((1, 1, S, 128), lambda b, g, i: (b, g, 0, 0)),
        pl.BlockSpec((1, S // C, VT_GROUP_ROWS, C), lambda b, g, i: (b, 0, g, 0)),
        pl.BlockSpec((1, 1, S, 128), lambda b, g, i: (b, g, 0, 0)),
        pl.BlockSpec((1, S // WIN_UNIT, VT_GROUP_ROWS, WIN_UNIT), lambda b, g, i: (b, 0, g, 0)),
        full(e_mat), full(kdead), full(movT), full(cbias), full(wbias), full(tbias),
    ]
    return pl.pallas_call(
        _nsa_kernel, grid=(B, G, S // T), in_specs=in_specs,
        out_specs=pl.BlockSpec((1, rows, T), lambda b, g, i: (b, g, i)),
        out_shape=jax.ShapeDtypeStruct((B, G * rows, S), BF16),
        scratch_shapes=[pltpu.VMEM((HPG, 256, T), BF16),
                        pltpu.VMEM((HPG, C, T), F32), pltpu.VMEM((HPG, C, T), F32),
                        pltpu.VMEM((HPG, C, T), BF16), pltpu.VMEM((HPG, C, T), BF16),
                        pltpu.VMEM((HPG, VT_GROUP_ROWS, T), F32),
                        pltpu.VMEM((HEAD_DIM, HPG * T), F32),
                        pltpu.VMEM((S // SLC_BLOCK, T), F32)],
        name="nsa",
        compiler_params=pltpu.CompilerParams(
            dimension_semantics=("parallel", "parallel", "arbitrary"),
            vmem_limit_bytes=VMEM_LIMIT),
    )(qT, gT, kc, vcT, ks, vsT, kw, vwT, e_mat, kdead, movT, cbias, wbias, tbias)


def _outproj_kernel(yc_ref, ynT_ref, x_ref, wc_ref, wn_ref, b_ref, g_ref, beta_ref, o_ref):
    mix = _dot(yc_ref[0], wc_ref[...]) + _dot_tn(ynT_ref[0], wn_ref[...])
    z = ALPHA * x_ref[0] + mix + b_ref[...]
    o_ref[0] = _layer_norm(z, g_ref[...], beta_ref[...])


def _outproj_call(yc, ynT, x, wc, wn, b, g, beta):
    B, S, D = x.shape
    tm = OUT_ROWS
    const = lambda b_, i: (0, 0)
    vec = pl.BlockSpec((1, D), const)
    return pl.pallas_call(
        _outproj_kernel, grid=(B, S // tm),
        in_specs=[
            pl.BlockSpec((1, tm, CONV_CH), lambda b_, i: (b_, i, 0)),
            pl.BlockSpec((1, D - CONV_CH, tm), lambda b_, i: (b_, 0, i)),
            pl.BlockSpec((1, tm, D), lambda b_, i: (b_, i, 0)),
            pl.BlockSpec((CONV_CH, D), const), pl.BlockSpec((D - CONV_CH, D), const),
            vec, vec, vec,
        ],
        out_specs=pl.BlockSpec((1, tm, D), lambda b_, i: (b_, i, 0)),
        out_shape=jax.ShapeDtypeStruct((B, S, D), F32),
        name="outproj",
        compiler_params=pltpu.CompilerParams(
            dimension_semantics=("parallel", "parallel"), vmem_limit_bytes=VMEM_LIMIT),
    )(yc, ynT, x, wc, wn, b, g, beta)


def _ffn_kernel(x_ref, p_ref, wup_ref, bup_ref, wdn_ref, bdn_ref, wpe_ref, wpg_ref,
                g_ref, beta_ref, o_ref, acc_ref):
    x1 = x_ref[...]
    xb = x1.astype(BF16)
    ple = _dot(p_ref[...].astype(BF16), wpe_ref[...]) * jax.nn.sigmoid(_dot(xb, wpg_ref[...]))
    acc_ref[...] = ALPHA * x1 + ple + bdn_ref[...]
    for c in range(0, D_FF, FFN_CHUNK):
        u = _dot(xb, wup_ref[:, c:c + FFN_CHUNK]) + bup_ref[:, c:c + FFN_CHUNK]
        u = jnp.square(jnp.maximum(u, 0.0)).astype(BF16)
        acc_ref[...] += _dot(u, wdn_ref[c:c + FFN_CHUNK, :])
    o_ref[...] = _layer_norm(acc_ref[...], g_ref[...], beta_ref[...])


def _ffn_call(x1, p, wup, bup, wdn, bdn, wpe, wpg, g, beta):
    N, D = x1.shape
    tm = FFN_ROWS
    const = lambda i: (0, 0)
    single = pl.Buffered(1)
    vec = pl.BlockSpec((1, D), const)
    return pl.pallas_call(
        _ffn_kernel, grid=(N // tm,),
        in_specs=[
            pl.BlockSpec((tm, D), lambda i: (i, 0)),
            pl.BlockSpec((tm, PLE_DIM), lambda i: (i, 0)),
            pl.BlockSpec((D, D_FF), const, pipeline_mode=single),
            pl.BlockSpec((1, D_FF), const),
            pl.BlockSpec((D_FF, D), const, pipeline_mode=single),
            vec,
            pl.BlockSpec((PLE_DIM, D), const, pipeline_mode=single),
            pl.BlockSpec((D, D), const, pipeline_mode=single),
            vec, vec,
        ],
        out_specs=pl.BlockSpec((tm, D), lambda i: (i, 0)),
        out_shape=jax.ShapeDtypeStruct((N, D), F32),
        scratch_shapes=[pltpu.VMEM((tm, D), F32)],
        name="ffn",
        compiler_params=pltpu.CompilerParams(
            dimension_semantics=("parallel",), vmem_limit_bytes=VMEM_LIMIT),
    )(x1, p, wup, bup, wdn, bdn, wpe, wpg, g, beta)


def _pad_groups(w):
    z = jnp.zeros(w.shape[:-1] + (HEAD_DIM,), w.dtype)
    return jnp.concatenate([w[..., :HEAD_DIM], z, w[..., HEAD_DIM:], z], axis=-1)


def _overlap_matrix_t(n_cmp_padded, n_slc):
    n_cmp = n_cmp_padded - 1
    c0 = np.arange(n_cmp) * CMP_STRIDE
    c1 = c0 + CMP_LEN - 1
    s0 = np.arange(n_slc) * SLC_BLOCK
    s1 = s0 + SLC_BLOCK - 1
    m = ((c0[:, None] <= s1[None, :]) & (c1[:, None] >= s0[None, :])).astype(np.float32)
    out = np.zeros((n_slc, n_cmp_padded), np.float32)
    out[:, :n_cmp] = m.T
    return out


def _vt_rows(w, bias):
    d = w.shape[0]
    pad = VT_GROUP_ROWS - HEAD_DIM
    one = jnp.zeros((pad,), F32).at[0].set(1.0)
    ws, bs = [], []
    for g in range(N_KV):
        ws += [w[:, g * HEAD_DIM:(g + 1) * HEAD_DIM], jnp.zeros((d, pad), F32)]
        bs += [bias[g * HEAD_DIM:(g + 1) * HEAD_DIM], one]
    return jnp.concatenate(ws, axis=1), jnp.concatenate(bs)


def _nsa_constants(S, n_cmp_p):
    T, C = Q_TILE, KEY_CHUNK
    t = np.arange(T)[None, :]
    neg = np.float32(NEG_INF)

    def bias(valid):
        return jnp.asarray(np.where(valid, np.float32(0.0), neg))

    e_mat = (np.arange(S)[:, None] // SLC_BLOCK == np.arange(S // SLC_BLOCK)[None, :])
    kdead = np.zeros((C, 128), np.float32)
    kdead[:, HEAD_DIM] = 1.0
    j = np.arange(2 * n_cmp_p)[:, None]
    cbias = bias(CMP_STRIDE * (j - n_cmp_p) + CMP_LEN - 1 <= t)
    j = np.arange(2 * WINDOW + T)[:, None]
    wbias = bias((t < j) & (j <= WINDOW + t))
    j = np.arange(2 * C - T)[:, None]
    tbias = bias(j - (C - T) <= t)
    return (jnp.asarray(e_mat.astype(np.float32), BF16), jnp.asarray(kdead, BF16),
            jnp.asarray(_overlap_matrix_t(n_cmp_p, S // SLC_BLOCK), BF16), cbias, wbias, tbias)


def _layer(x, p, w_in, b_in, conv_dw_w, conv_dw_b, conv_ln_g, conv_ln_b,
           cmp_pos_k, cmp_w1_k, cmp_w2_k, cmp_pos_v, cmp_w1_v, cmp_w2_v,
           w_out, b_out, ln1_g, ln1_b, w_up, b_up, w_down, b_down, w_pe, w_pg, ln2_g, ln2_b):
    B, S, D = x.shape
    scale = HEAD_DIM ** -0.5 * LOG2_E
    row = lambda v: v.reshape(1, -1).astype(F32)

    w, bias = w_in, b_in
    wrm = jnp.concatenate(
        [w[:, 0:1024], w[:, 1536:1792], _pad_groups(w[:, 1792:1920]), _pad_groups(w[:, 2048:2176])],
        axis=1).astype(BF16)
    brm = row(jnp.concatenate(
        [bias[0:1024], bias[1536:1792], _pad_groups(bias[1792:1920]), _pad_groups(bias[2048:2176])]))
    w_vs, b_vs = _vt_rows(w[:, 1920:2048], bias[1920:2048])
    w_vw, b_vw = _vt_rows(w[:, 2176:2304], bias[2176:2304])
    wt = jnp.concatenate(
        [w[:, 1024:1536] * scale, w_vs, w_vw, w[:, 2304:2328], jnp.zeros((D, 8), F32)],
        axis=1).T.astype(BF16)
    bt = jnp.concatenate(
        [bias[1024:1536] * scale, b_vs, b_vw, bias[2304:2328], jnp.zeros((8,), F32)]).reshape(-1, 1)

    vglu, kc, vc, ks, kw, qT, vsT, vwT, gT = _proj_call(x, wrm, brm, wt, bt)

    conv_w = jnp.broadcast_to(conv_dw_w.reshape(CONV_WIDTH, 1, CONV_CH),
                              (CONV_WIDTH, SUBLANES, CONV_CH))
    y_conv = _conv_call(vglu, conv_w, row(conv_dw_b), row(conv_ln_g), row(conv_ln_b))

    n_cmp_p = S // CMP_STRIDE
    blk_w = CMP_STRIDE * HEAD_DIM
    rk = kc.reshape(B, N_KV, n_cmp_p, blk_w)
    rv = vc.reshape(B, N_KV, n_cmp_p, blk_w)
    w2k = jnp.concatenate([cmp_w2_k, jnp.zeros((CMP_HIDDEN, 128 - HEAD_DIM), F32)], axis=1)
    kcz, vcT = _compress_call(
        rk, rv, cmp_pos_k.reshape(2, blk_w), cmp_w1_k.astype(BF16), w2k.astype(BF16),
        cmp_pos_v.reshape(2, blk_w), cmp_w1_v.astype(BF16), cmp_w2_v.T.astype(BF16))

    y_nsaT = _nsa_call(qT, gT, kcz, vcT, ks, vsT, kw, vwT, _nsa_constants(S, n_cmp_p))

    wo = w_out.astype(BF16)
    x1 = _outproj_call(y_conv, y_nsaT, x, wo[:CONV_CH], wo[CONV_CH:], row(b_out),
                       row(ln1_g), row(ln1_b))

    out = _ffn_call(x1.reshape(B * S, D), p.reshape(B * S, PLE_DIM),
                    w_up.astype(BF16), row(b_up), w_down.astype(BF16), row(b_down),
                    w_pe.astype(BF16), w_pg.astype(BF16), row(ln2_g), row(ln2_b))
    return out.reshape(B, S, D)


def kernel(x, p, w_in, b_in, conv_dw_w, conv_dw_b, conv_ln_g, conv_ln_b, cmp_pos_k, cmp_w1_k, cmp_w2_k, cmp_pos_v, cmp_w1_v, cmp_w2_v, w_out, b_out, ln1_g, ln1_b, w_up, b_up, w_down, b_down, w_pe, w_pg, ln2_g, ln2_b):
    params = (w_in, b_in, conv_dw_w, conv_dw_b, conv_ln_g, conv_ln_b,
              cmp_pos_k, cmp_w1_k, cmp_w2_k, cmp_pos_v, cmp_w1_v, cmp_w2_v,
              w_out, b_out, ln1_g, ln1_b, w_up, b_up, w_down, b_down, w_pe, w_pg, ln2_g, ln2_b)
    for i in range(DEPTH):
        x = _layer(x, p[i], *[t[i] for t in params])
    return x
```

```python
import functools

import jax
import jax.numpy as jnp
import numpy as np
from jax import lax
from jax.experimental import pallas as pl
from jax.experimental.pallas import tpu as pltpu

F32 = jnp.float32
BF16 = jnp.bfloat16

D_MODEL = 1024
PLE_DIM = 256
CONV_CH = 512
CONV_WIDTH = 31
HEAD_DIM = 64
N_KV = 2
HPG = 4
N_BRANCH = 3
CMP_LEN = 32
CMP_STRIDE = 16
CMP_HIDDEN = 256
SLC_BLOCK = 64
SLC_TOPN = 16
WINDOW = 512
D_FF = 4 * D_MODEL
LN_EPS = 1e-5
NEG_INF = -1e30
FORCE_BONUS = 1e4
DEPTH = 1
ALPHA = (2 * DEPTH) ** 0.25

UNSELECTED_BIAS = -float(2 ** 30)
BELOW_NEG_INF = -3e38

SUBLANES = 8
PROJ_ROWS = 512
CONV_ROWS = 256
CONV_HALO = 32
CONV_ROW_CHUNK = 32
Q_TILE = 256
KEY_CHUNK = 256
WIN_UNIT = 128
CMP_UNIT = 128
LOG2_E = 1.4426950408889634
OUT_ROWS = 512
FFN_ROWS = 512
FFN_CHUNK = 1024
VMEM_LIMIT = 56 * 1024 * 1024

RM_COLS = 1792
VT_GROUP_ROWS = 80
VT_ROWS = N_KV * VT_GROUP_ROWS
T_ROWS = 512 + 2 * VT_ROWS + 32


def _layer_norm(z, g, b):
    mu = jnp.mean(z, axis=-1, keepdims=True)
    zc = z - mu
    var = jnp.mean(zc * zc, axis=-1, keepdims=True)
    return zc * lax.rsqrt(var + LN_EPS) * g + b


def _dot(a, b):
    return jnp.dot(a, b, preferred_element_type=F32)


def _dot_nt(a, b):
    return lax.dot_general(a, b, (((1,), (1,)), ((), ())), preferred_element_type=F32)


def _dot_tn(a, b):
    return lax.dot_general(a, b, (((0,), (0,)), ((), ())), preferred_element_type=F32)


def _proj_kernel(x_ref, wrm_ref, brm_ref, wt_ref, bt_ref,
                 vglu_ref, kc_ref, vc_ref, ks_ref, kw_ref,
                 qT_ref, vsT_ref, vwT_ref, gT_ref):
    xb = x_ref[0].astype(BF16)

    def rm(c0, c1):
        return _dot(xb, wrm_ref[:, c0:c1]) + brm_ref[:, c0:c1]

    half = CONV_CH // 2
    for c in range(0, CONV_CH, half):
        a = rm(c, c + half)
        g = rm(CONV_CH + c, CONV_CH + c + half)
        vglu_ref[0, :, c:c + half] = a * jax.nn.sigmoid(g)

    kv = rm(1024, 1280)
    kc_ref[0, 0] = kv[:, 0:64]
    kc_ref[0, 1] = kv[:, 64:128]
    vc_ref[0, 0] = kv[:, 128:192]
    vc_ref[0, 1] = kv[:, 192:256]

    kz = rm(1280, 1792).astype(BF16)
    ks_ref[0, 0] = kz[:, 0:128]
    ks_ref[0, 1] = kz[:, 128:256]
    kw_ref[0, 0] = kz[:, 256:384]
    kw_ref[0, 1] = kz[:, 384:512]

    def tr(r0, r1):
        return _dot_nt(wt_ref[r0:r1, :], xb) + bt_ref[r0:r1, :]

    qT_ref[0] = tr(0, 512).astype(BF16)
    r = 512
    vs = tr(r, r + VT_ROWS).astype(BF16)
    for u in range(PROJ_ROWS // KEY_CHUNK):
        vsT_ref[0, u] = vs[:, u * KEY_CHUNK:(u + 1) * KEY_CHUNK]
    r += VT_ROWS
    vw = tr(r, r + VT_ROWS).astype(BF16)
    for u in range(PROJ_ROWS // WIN_UNIT):
        vwT_ref[0, u] = vw[:, u * WIN_UNIT:(u + 1) * WIN_UNIT]
    r += VT_ROWS
    gT_ref[0] = jax.nn.sigmoid(tr(r, r + 32))


def _proj_call(x, wrm, brm, wt, bt):
    B, S, D = x.shape
    tm = PROJ_ROWS
    n = S // tm
    const = lambda b, i: (0, 0)
    out_shape = (
        jax.ShapeDtypeStruct((B, S, CONV_CH), F32),
        jax.ShapeDtypeStruct((B, N_KV, S, HEAD_DIM), F32),
        jax.ShapeDtypeStruct((B, N_KV, S, HEAD_DIM), F32),
        jax.ShapeDtypeStruct((B, N_KV, S, 128), BF16),
        jax.ShapeDtypeStruct((B, N_KV, S, 128), BF16),
        jax.ShapeDtypeStruct((B, 512, S), BF16),
        jax.ShapeDtypeStruct((B, S // KEY_CHUNK, VT_ROWS, KEY_CHUNK), BF16),
        jax.ShapeDtypeStruct((B, S // WIN_UNIT, VT_ROWS, WIN_UNIT), BF16),
        jax.ShapeDtypeStruct((B, 32, S), F32),
    )
    kvspec = pl.BlockSpec((1, N_KV, tm, HEAD_DIM), lambda b, i: (b, 0, i, 0))
    kzspec = pl.BlockSpec((1, N_KV, tm, 128), lambda b, i: (b, 0, i, 0))
    out_specs = (
        pl.BlockSpec((1, tm, CONV_CH), lambda b, i: (b, i, 0)),
        kvspec, kvspec, kzspec, kzspec,
        pl.BlockSpec((1, 512, tm), lambda b, i: (b, 0, i)),
        pl.BlockSpec((1, tm // KEY_CHUNK, VT_ROWS, KEY_CHUNK), lambda b, i: (b, i, 0, 0)),
        pl.BlockSpec((1, tm // WIN_UNIT, VT_ROWS, WIN_UNIT), lambda b, i: (b, i, 0, 0)),
        pl.BlockSpec((1, 32, tm), lambda b, i: (b, 0, i)),
    )
    in_specs = [
        pl.BlockSpec((1, tm, D), lambda b, i: (b, i, 0)),
        pl.BlockSpec((D, RM_COLS), const),
        pl.BlockSpec((1, RM_COLS), const),
        pl.BlockSpec((T_ROWS, D), const),
        pl.BlockSpec((T_ROWS, 1), const),
    ]
    return pl.pallas_call(
        _proj_kernel, grid=(B, n), in_specs=in_specs, out_specs=out_specs,
        out_shape=out_shape, name="proj",
        compiler_params=pltpu.CompilerParams(
            dimension_semantics=("parallel", "parallel"), vmem_limit_bytes=VMEM_LIMIT),
    )(x, wrm, brm, wt, bt)


def _conv_kernel(cur_ref, halo_ref, w_ref, b_ref, g_ref, beta_ref, o_ref, xs_ref):
    i = pl.program_id(1)
    n_shift = xs_ref.shape[1]
    xs_ref[0, 0:CONV_HALO, :] = jnp.where(i > 0, halo_ref[0], 0.0)
    xs_ref[0, CONV_HALO:, :] = cur_ref[0]
    x0 = xs_ref[0]
    for b in range(1, SUBLANES):
        xs_ref[b] = pltpu.roll(x0, n_shift - b, axis=0)
    lead = CONV_HALO - (CONV_WIDTH - 1)
    for r in range(0, CONV_ROWS, CONV_ROW_CHUNK):
        acc = jnp.zeros((CONV_ROW_CHUNK, CONV_CH), F32)
        for k in range(CONV_WIDTH):
            a, b = divmod(lead + k, SUBLANES)
            row = r + a * SUBLANES
            w_k = jnp.concatenate([w_ref[k]] * (CONV_ROW_CHUNK // SUBLANES), axis=0)
            acc = acc + xs_ref[b, row:row + CONV_ROW_CHUNK, :] * w_k
        y = _layer_norm(acc + b_ref[...], g_ref[...], beta_ref[...])
        o_ref[0, r:r + CONV_ROW_CHUNK, :] = (y * jax.nn.sigmoid(y)).astype(BF16)


def _conv_call(v, w, b, g, beta):
    B, S, C = v.shape
    tc = CONV_ROWS
    ratio = tc // CONV_HALO
    const = lambda b_, i: (0, 0)
    return pl.pallas_call(
        _conv_kernel, grid=(B, S // tc),
        in_specs=[
            pl.BlockSpec((1, tc, C), lambda b_, i: (b_, i, 0)),
            pl.BlockSpec((1, CONV_HALO, C), lambda b_, i: (b_, jnp.maximum(i * ratio - 1, 0), 0)),
            pl.BlockSpec((CONV_WIDTH, SUBLANES, C), lambda b_, i: (0, 0, 0)),
            pl.BlockSpec((1, C), const), pl.BlockSpec((1, C), const), pl.BlockSpec((1, C), const),
        ],
        out_specs=pl.BlockSpec((1, tc, C), lambda b_, i: (b_, i, 0)),
        out_shape=jax.ShapeDtypeStruct((B, S, C), BF16),
        scratch_shapes=[pltpu.VMEM((SUBLANES, tc + CONV_HALO, C), F32)],
        name="conv",
        compiler_params=pltpu.CompilerParams(
            dimension_semantics=("parallel", "parallel"), vmem_limit_bytes=VMEM_LIMIT),
    )(v, v, w, b, g, beta)


def _compress_hidden(r, pos_ref, w1_ref):
    half = CMP_STRIDE * HEAD_DIM
    n_rows = r.shape[0]
    a = _dot((r + pos_ref[0:1, :]).astype(BF16), w1_ref[0:half, :])
    b = _dot((r + pos_ref[1:2, :]).astype(BF16), w1_ref[half:2 * half, :])
    h = a + pltpu.roll(b, n_rows - 1, axis=0)
    return (h * jax.nn.sigmoid(h)).astype(BF16)


def _compress_kernel(rk_ref, rv_ref, pk_ref, w1k_ref, w2k_ref, pv_ref, w1v_ref, w2vT_ref,
                     kc_ref, vcT_ref):
    hk = _compress_hidden(rk_ref[0, 0], pk_ref, w1k_ref)
    kc_ref[0, 0] = _dot(hk, w2k_ref[...]).astype(BF16)
    hv = _compress_hidden(rv_ref[0, 0], pv_ref, w1v_ref)
    vcT_ref[0, 0] = _dot_nt(w2vT_ref[...], hv).astype(BF16)


def _compress_call(rk, rv, pk, w1k, w2k, pv, w1v, w2vT):
    B, G, NC, W = rk.shape
    const = lambda b, g: (0, 0)
    rspec = pl.BlockSpec((1, 1, NC, W), lambda b, g: (b, g, 0, 0))
    return pl.pallas_call(
        _compress_kernel, grid=(B, G),
        in_specs=[
            rspec, rspec,
            pl.BlockSpec((2, W), const), pl.BlockSpec((2 * W, CMP_HIDDEN), const),
            pl.BlockSpec((CMP_HIDDEN, 128), const),
            pl.BlockSpec((2, W), const), pl.BlockSpec((2 * W, CMP_HIDDEN), const),
            pl.BlockSpec((HEAD_DIM, CMP_HIDDEN), const),
        ],
        out_specs=(
            pl.BlockSpec((1, 1, NC, 128), lambda b, g: (b, g, 0, 0)),
            pl.BlockSpec((1, 1, HEAD_DIM, NC), lambda b, g: (b, g, 0, 0)),
        ),
        out_shape=(
            jax.ShapeDtypeStruct((B, G, NC, 128), BF16),
            jax.ShapeDtypeStruct((B, G, HEAD_DIM, NC), BF16),
        ),
        name="compress",
        compiler_params=pltpu.CompilerParams(
            dimension_semantics=("parallel", "parallel"), vmem_limit_bytes=VMEM_LIMIT),
    )(rk, rv, pk, w1k, w2k, pv, w1v, w2vT)


def _nsa_kernel(q_ref, g_ref, kc_ref, vcT_ref, ks_ref, vsT_ref, kw_ref, vwT_ref,
                e_ref, kdead_ref, mov_ref, cbias_ref, wbias_ref, tbias_ref,
                o_ref, q2_ref, s_a, s_b, p_a, p_b, acc_ref, oc_ref, imp_ref):
    T = Q_TILE
    C = KEY_CHUNK
    qb = pl.program_id(1)
    t0 = qb * T
    n_cmp = kc_ref.shape[2]
    n_slc = mov_ref.shape[0]
    n_chunks = vsT_ref.shape[1]
    groups = range(N_KV)
    heads = range(HPG)
    cols = [(g, h) for g in groups for h in heads]

    def col(g, h):
        return g * HPG + h

    q4 = q_ref[0]
    spare = jnp.where(lax.broadcasted_iota(jnp.int32, (HEAD_DIM, T), 0) == 0,
                      UNSELECTED_BIAS, 0.0).astype(BF16)
    q1 = [jnp.concatenate([q4[i * HEAD_DIM:(i + 1) * HEAD_DIM, :], spare], axis=0)
          for i in range(len(cols))]
    q1_all = [jnp.concatenate(q1[g * HPG:(g + 1) * HPG], axis=1) for g in groups]
    pos_t = t0 + lax.broadcasted_iota(jnp.int32, (1, T), 1)

    c_off = pl.multiple_of(n_cmp - qb * (T // CMP_STRIDE), T // CMP_STRIDE)

    def all_heads(a):
        return jnp.concatenate([a] * HPG, axis=1)

    def head_cols(a):
        return [a[:, h * T:(h + 1) * T] for h in heads]

    def compressed(rows):
        for g in groups:
            sc = (_dot(kc_ref[0, g, 0:rows, :], q1_all[g])
                  + all_heads(cbias_ref[pl.ds(c_off, rows), :]))
            m_c = jnp.max(sc, axis=0, keepdims=True)
            p_c = jnp.exp2(sc - m_c)
            l_c = jnp.sum(p_c, axis=0, keepdims=True)
            p_c = p_c * jnp.where(m_c > 0.5 * NEG_INF, 1.0 / l_c, 0.0)
            oc_ref[g] = _dot(vcT_ref[0, g, :, 0:rows], p_c.astype(BF16))
            psum = functools.reduce(lambda a, b: a + b, head_cols(p_c))
            mov = mov_ref[:, 0:rows]
            p_hi = psum.astype(BF16)
            rem = psum - p_hi.astype(F32)
            p_mid = rem.astype(BF16)
            p_lo = (rem - p_mid.astype(F32)).astype(BF16)
            imp_ref[g] = _dot(mov, p_hi) + _dot(mov, p_mid) + _dot(mov, p_lo)

    cmp_units = (t0 + T - CMP_LEN) // CMP_STRIDE // CMP_UNIT + 1
    for units in range(1, n_cmp // CMP_UNIT + 1):
        pl.when(cmp_units == units)(functools.partial(compressed, units * CMP_UNIT))

    w_keys = WINDOW + T
    start = pl.multiple_of(jnp.maximum(t0 - WINDOW, 0), WIN_UNIT)
    w_off = pl.multiple_of(WINDOW - (t0 - start), WIN_UNIT)
    u0 = start // WIN_UNIT
    o_w = []
    for g in groups:
        v_win = jnp.concatenate(
            [vwT_ref[0, u0 + u, g * VT_GROUP_ROWS:(g + 1) * VT_GROUP_ROWS, :]
             for u in range(w_keys // WIN_UNIT)], axis=1)
        sw = (_dot(kw_ref[0, g, pl.ds(start, w_keys), :], q1_all[g])
              + all_heads(wbias_ref[pl.ds(w_off, w_keys), :]))
        p_w = jnp.exp2(sw - jnp.max(sw, axis=0, keepdims=True)).astype(BF16)
        ow = _dot(v_win, p_w)
        o_w += head_cols(ow[0:HEAD_DIM] * (1.0 / ow[HEAD_DIM:HEAD_DIM + 1]))

    blk = lax.broadcasted_iota(jnp.int32, (n_slc, T), 0)
    blk_f = blk.astype(F32)
    cur = jnp.right_shift(pos_t, 6)
    forced = (blk == 0) | (blk == cur) | (blk == cur - 1)
    valid_b = (blk * SLC_BLOCK) <= pos_t
    bonus = jnp.where(forced, FORCE_BONUS, 0.0)
    work = [jnp.where(valid_b, imp_ref[g] + bonus, NEG_INF) for g in groups]
    for _ in range(SLC_TOPN):
        for g in groups:
            mx = jnp.max(work[g], axis=0, keepdims=True)
            first = jnp.min(jnp.where(work[g] == mx, blk_f, float(n_slc)), axis=0, keepdims=True)
            work[g] = jnp.where(blk_f == first, BELOW_NEG_INF, work[g])
    for g in groups:
        picked = work[g] < 0.5 * BELOW_NEG_INF
        unsel = jnp.where(picked & valid_b, 0.0, UNSELECTED_BIAS).astype(BF16)
        for h in heads:
            q2_ref[col(g, h)] = jnp.concatenate([q1[col(g, h)], unsel], axis=0)

    n_full = t0 // C
    uq = qb - n_full * (C // T)

    def key_operands(c, live):
        c = jnp.minimum(c, n_chunks - 1)
        off = pl.multiple_of(c * C, C)
        e_c = e_ref[pl.ds(off, C), :]
        out = []
        for g in groups:
            ks_c = ks_ref[0, g, pl.ds(off, C), :]
            if live is not None:
                ks_c = jnp.where(live, ks_c, kdead_ref[...])
            out.append(jnp.concatenate([ks_c, e_c], axis=1))
        return out

    def values(c):
        return [vsT_ref[0, c, g * VT_GROUP_ROWS:(g + 1) * VT_GROUP_ROWS, :] for g in groups]

    def scores(i, k2, sbuf, bias=None):
        s = _dot(k2, q2_ref[i])
        if bias is not None:
            s = s + bias
        sbuf[i] = s
        return jnp.max(s, axis=0, keepdims=True)

    def weights(i, sbuf, pbuf, cm, m_old):
        m_new = jnp.maximum(m_old, cm)
        pbuf[i] = jnp.exp2(sbuf[i] - m_new).astype(BF16)
        return m_new, jnp.exp2(m_old - m_new)

    def add_values(i, vt, pbuf, rescale):
        acc_ref[i] = rescale * acc_ref[i] + _dot(vt, pbuf[i])

    k2_t = key_operands(n_full, None)
    k2_0 = key_operands(0, 0 < n_full)
    t_off = pl.multiple_of((C // T - 1 - uq) * T, T)
    m0, cm0 = [], []
    for g, h in cols:
        i = col(g, h)
        cm_t = scores(i, k2_t[g], s_a, tbias_ref[pl.ds(t_off, C), :])
        cm0.append(scores(i, k2_0[g], s_b))
        acc_ref[i] = jnp.zeros((VT_GROUP_ROWS, T), F32)
        m0.append(weights(i, s_a, p_a, cm_t, cm_t)[0])

    def pair_body(trip, carry):
        m, scale_pend, c_pend, cm_b = carry
        c0 = 2 * trip
        k2_a = key_operands(c0 + 1, c0 + 1 < n_full)
        k2_b = key_operands(c0 + 2, c0 + 2 < n_full)
        vt_pend = values(c_pend)
        vt_0 = values(c0)
        m1, scale0, cm_a = [], [], []
        for g, h in cols:
            i = col(g, h)
            add_values(i, vt_pend[g], p_a, scale_pend[i])
            cm_a.append(scores(i, k2_a[g], s_a))
            m_i, s_i = weights(i, s_b, p_b, cm_b[i], m[i])
            m1.append(m_i)
            scale0.append(s_i)
        m2, scale1, cm_b2 = [], [], []
        for g, h in cols:
            i = col(g, h)
            cm_b2.append(scores(i, k2_b[g], s_b))
            add_values(i, vt_0[g], p_b, scale0[i])
            m_i, s_i = weights(i, s_a, p_a, cm_a[i], m1[i])
            m2.append(m_i)
            scale1.append(s_i)
        return tuple(m2), tuple(scale1), jnp.minimum(c0 + 1, n_chunks - 1), tuple(cm_b2)

    ones = tuple(jnp.ones((1, T), F32) for _ in cols)
    _, scale_pend, c_pend, _ = lax.fori_loop(
        0, (n_full + 1) // 2, pair_body, (tuple(m0), ones, n_full, tuple(cm0)))

    vt_pend = values(c_pend)
    for g, h in cols:
        i = col(g, h)
        add_values(i, vt_pend[g], p_a, scale_pend[i])
        acc = acc_ref[i]
        o_s = acc[0:HEAD_DIM] * (1.0 / acc[HEAD_DIM:HEAD_DIM + 1])
        o_c = oc_ref[g, :, h * T:(h + 1) * T]
        gate = [g_ref[0, pl.ds(i * N_BRANCH + br, 1), :] for br in range(N_BRANCH)]
        out = gate[0] * o_c + gate[1] * o_s + gate[2] * o_w[i]
        o_ref[0, i * HEAD_DIM:(i + 1) * HEAD_DIM, :] = out.astype(BF16)


def _nsa_call(qT, gT, kc, vcT, ks, vsT, kw, vwT, consts):
    B, _, S = qT.shape
    G = N_KV
    T = Q_TILE
    C = KEY_CHUNK
    n_cmp = kc.shape[2]
    rows = G * HPG * HEAD_DIM
    n_cols = G * HPG
    e_mat, kdead, movT, cbias, wbias, tbias = consts
    once = pl.Buffered(1)
    per_batch = lambda shape: pl.BlockSpec((1,) + shape, lambda b, i: (b, 0, 0, 0), pipeline_mode=once)
    full = lambda a: pl.BlockSpec(a.shape, lambda b, i: (0, 0), pipeline_mode=once)
    in_specs = [
        pl.BlockSpec((1, rows, T), lambda b, i: (b, 0, i)),
        pl.BlockSpec((1, 32, T), lambda b, i: (b, 0, i)),
        per_batch((G, n_cmp, 128)),
        per_batch((G, HEAD_DIM, n_cmp)),
        per_batch((G, S, 128)),
        per_batch((S // C, VT_ROWS, C)),
        per_batch((G, S, 128)),
        per_batch((S // WIN_UNIT, VT_ROWS, WIN_UNIT)),
        full(e_mat), full(kdead), full(movT), full(cbias), full(wbias), full(tbias),
    ]
    return pl.pallas_call(
        _nsa_kernel, grid=(B, S // T), in_specs=in_specs,
        out_specs=pl.BlockSpec((1, rows, T), lambda b, i: (b, 0, i)),
        out_shape=jax.ShapeDtypeStruct((B, rows, S), BF16),
        scratch_shapes=[pltpu.VMEM((n_cols, 256, T), BF16),
                        pltpu.VMEM((n_cols, C, T), F32), pltpu.VMEM((n_cols, C, T), F32),
                        pltpu.VMEM((n_cols, C, T), BF16), pltpu.VMEM((n_cols, C, T), BF16),
                        pltpu.VMEM((n_cols, VT_GROUP_ROWS, T), F32),
                        pltpu.VMEM((G, HEAD_DIM, HPG * T), F32),
                        pltpu.VMEM((G, S // SLC_BLOCK, T), F32)],
        name="nsa",
        compiler_params=pltpu.CompilerParams(
            dimension_semantics=("parallel", "arbitrary"),
            vmem_limit_bytes=VMEM_LIMIT),
    )(qT, gT, kc, vcT, ks, vsT, kw, vwT, e_mat, kdead, movT, cbias, wbias, tbias)


def _outproj_kernel(yc_ref, ynT_ref, x_ref, wc_ref, wn_ref, b_ref, g_ref, beta_ref, o_ref):
    mix = _dot(yc_ref[0], wc_ref[...]) + _dot_tn(ynT_ref[0], wn_ref[...])
    z = ALPHA * x_ref[0] + mix + b_ref[...]
    o_ref[0] = _layer_norm(z, g_ref[...], beta_ref[...])


def _outproj_call(yc, ynT, x, wc, wn, b, g, beta):
    B, S, D = x.shape
    tm = OUT_ROWS
    const = lambda b_, i: (0, 0)
    vec = pl.BlockSpec((1, D), const)
    return pl.pallas_call(
        _outproj_kernel, grid=(B, S // tm),
        in_specs=[
            pl.BlockSpec((1, tm, CONV_CH), lambda b_, i: (b_, i, 0)),
            pl.BlockSpec((1, D - CONV_CH, tm), lambda b_, i: (b_, 0, i)),
            pl.BlockSpec((1, tm, D), lambda b_, i: (b_, i, 0)),
            pl.BlockSpec((CONV_CH, D), const), pl.BlockSpec((D - CONV_CH, D), const),
            vec, vec, vec,
        ],
        out_specs=pl.BlockSpec((1, tm, D), lambda b_, i: (b_, i, 0)),
        out_shape=jax.ShapeDtypeStruct((B, S, D), F32),
        name="outproj",
        compiler_params=pltpu.CompilerParams(
            dimension_semantics=("parallel", "parallel"), vmem_limit_bytes=VMEM_LIMIT),
    )(yc, ynT, x, wc, wn, b, g, beta)


def _ffn_kernel(x_ref, p_ref, wup_ref, bup_ref, wdn_ref, bdn_ref, wpe_ref, wpg_ref,
                g_ref, beta_ref, o_ref, acc_ref):
    x1 = x_ref[...]
    xb = x1.astype(BF16)
    ple = _dot(p_ref[...].astype(BF16), wpe_ref[...]) * jax.nn.sigmoid(_dot(xb, wpg_ref[...]))
    acc_ref[...] = ALPHA * x1 + ple + bdn_ref[...]
    for c in range(0, D_FF, FFN_CHUNK):
        u = _dot(xb, wup_ref[:, c:c + FFN_CHUNK]) + bup_ref[:, c:c + FFN_CHUNK]
        u = jnp.square(jnp.maximum(u, 0.0)).astype(BF16)
        acc_ref[...] += _dot(u, wdn_ref[c:c + FFN_CHUNK, :])
    o_ref[...] = _layer_norm(acc_ref[...], g_ref[...], beta_ref[...])


def _ffn_call(x1, p, wup, bup, wdn, bdn, wpe, wpg, g, beta):
    N, D = x1.shape
    tm = FFN_ROWS
    const = lambda i: (0, 0)
    single = pl.Buffered(1)
    vec = pl.BlockSpec((1, D), const)
    return pl.pallas_call(
        _ffn_kernel, grid=(N // tm,),
        in_specs=[
            pl.BlockSpec((tm, D), lambda i: (i, 0)),
            pl.BlockSpec((tm, PLE_DIM), lambda i: (i, 0)),
            pl.BlockSpec((D, D_FF), const, pipeline_mode=single),
            pl.BlockSpec((1, D_FF), const),
            pl.BlockSpec((D_FF, D), const, pipeline_mode=single),
            vec,
            pl.BlockSpec((PLE_DIM, D), const, pipeline_mode=single),
            pl.BlockSpec((D, D), const, pipeline_mode=single),
            vec, vec,
        ],
        out_specs=pl.BlockSpec((tm, D), lambda i: (i, 0)),
        out_shape=jax.ShapeDtypeStruct((N, D), F32),
        scratch_shapes=[pltpu.VMEM((tm, D), F32)],
        name="ffn",
        compiler_params=pltpu.CompilerParams(
            dimension_semantics=("parallel",), vmem_limit_bytes=VMEM_LIMIT),
    )(x1, p, wup, bup, wdn, bdn, wpe, wpg, g, beta)


def _pad_groups(w):
    z = jnp.zeros(w.shape[:-1] + (HEAD_DIM,), w.dtype)
    return jnp.concatenate([w[..., :HEAD_DIM], z, w[..., HEAD_DIM:], z], axis=-1)


def _overlap_matrix_t(n_cmp_padded, n_slc):
    n_cmp = n_cmp_padded - 1
    c0 = np.arange(n_cmp) * CMP_STRIDE
    c1 = c0 + CMP_LEN - 1
    s0 = np.arange(n_slc) * SLC_BLOCK
    s1 = s0 + SLC_BLOCK - 1
    m = ((c0[:, None] <= s1[None, :]) & (c1[:, None] >= s0[None, :])).astype(np.float32)
    out = np.zeros((n_slc, n_cmp_padded), np.float32)
    out[:, :n_cmp] = m.T
    return out


def _vt_rows(w, bias):
    d = w.shape[0]
    pad = VT_GROUP_ROWS - HEAD_DIM
    one = jnp.zeros((pad,), F32).at[0].set(1.0)
    ws, bs = [], []
    for g in range(N_KV):
        ws += [w[:, g * HEAD_DIM:(g + 1) * HEAD_DIM], jnp.zeros((d, pad), F32)]
        bs += [bias[g * HEAD_DIM:(g + 1) * HEAD_DIM], one]
    return jnp.concatenate(ws, axis=1), jnp.concatenate(bs)


def _nsa_constants(S, n_cmp_p):
    T, C = Q_TILE, KEY_CHUNK
    t = np.arange(T)[None, :]
    neg = np.float32(NEG_INF)

    def bias(valid):
        return jnp.asarray(np.where(valid, np.float32(0.0), neg))

    e_mat = (np.arange(S)[:, None] // SLC_BLOCK == np.arange(S // SLC_BLOCK)[None, :])
    kdead = np.zeros((C, 128), np.float32)
    kdead[:, HEAD_DIM] = 1.0
    j = np.arange(2 * n_cmp_p)[:, None]
    cbias = bias(CMP_STRIDE * (j - n_cmp_p) + CMP_LEN - 1 <= t)
    j = np.arange(2 * WINDOW + T)[:, None]
    wbias = bias((t < j) & (j <= WINDOW + t))
    j = np.arange(2 * C - T)[:, None]
    tbias = bias(j - (C - T) <= t)
    return (jnp.asarray(e_mat.astype(np.float32), BF16), jnp.asarray(kdead, BF16),
            jnp.asarray(_overlap_matrix_t(n_cmp_p, S // SLC_BLOCK), BF16), cbias, wbias, tbias)


def _layer(x, p, w_in, b_in, conv_dw_w, conv_dw_b, conv_ln_g, conv_ln_b,
           cmp_pos_k, cmp_w1_k, cmp_w2_k, cmp_pos_v, cmp_w1_v, cmp_w2_v,
           w_out, b_out, ln1_g, ln1_b, w_up, b_up, w_down, b_down, w_pe, w_pg, ln2_g, ln2_b):
    B, S, D = x.shape
    scale = HEAD_DIM ** -0.5 * LOG2_E
    row = lambda v: v.reshape(1, -1).astype(F32)

    w, bias = w_in, b_in
    wrm = jnp.concatenate(
        [w[:, 0:1024], w[:, 1536:1792], _pad_groups(w[:, 1792:1920]), _pad_groups(w[:, 2048:2176])],
        axis=1).astype(BF16)
    brm = row(jnp.concatenate(
        [bias[0:1024], bias[1536:1792], _pad_groups(bias[1792:1920]), _pad_groups(bias[2048:2176])]))
    w_vs, b_vs = _vt_rows(w[:, 1920:2048], bias[1920:2048])
    w_vw, b_vw = _vt_rows(w[:, 2176:2304], bias[2176:2304])
    wt = jnp.concatenate(
        [w[:, 1024:1536] * scale, w_vs, w_vw, w[:, 2304:2328], jnp.zeros((D, 8), F32)],
        axis=1).T.astype(BF16)
    bt = jnp.concatenate(
        [bias[1024:1536] * scale, b_vs, b_vw, bias[2304:2328], jnp.zeros((8,), F32)]).reshape(-1, 1)

    vglu, kc, vc, ks, kw, qT, vsT, vwT, gT = _proj_call(x, wrm, brm, wt, bt)

    conv_w = jnp.broadcast_to(conv_dw_w.reshape(CONV_WIDTH, 1, CONV_CH),
                              (CONV_WIDTH, SUBLANES, CONV_CH))
    y_conv = _conv_call(vglu, conv_w, row(conv_dw_b), row(conv_ln_g), row(conv_ln_b))

    n_cmp_p = S // CMP_STRIDE
    blk_w = CMP_STRIDE * HEAD_DIM
    rk = kc.reshape(B, N_KV, n_cmp_p, blk_w)
    rv = vc.reshape(B, N_KV, n_cmp_p, blk_w)
    w2k = jnp.concatenate([cmp_w2_k, jnp.zeros((CMP_HIDDEN, 128 - HEAD_DIM), F32)], axis=1)
    kcz, vcT = _compress_call(
        rk, rv, cmp_pos_k.reshape(2, blk_w), cmp_w1_k.astype(BF16), w2k.astype(BF16),
        cmp_pos_v.reshape(2, blk_w), cmp_w1_v.astype(BF16), cmp_w2_v.T.astype(BF16))

    y_nsaT = _nsa_call(qT, gT, kcz, vcT, ks, vsT, kw, vwT, _nsa_constants(S, n_cmp_p))

    wo = w_out.astype(BF16)
    x1 = _outproj_call(y_conv, y_nsaT, x, wo[:CONV_CH], wo[CONV_CH:], row(b_out),
                       row(ln1_g), row(ln1_b))

    out = _ffn_call(x1.reshape(B * S, D), p.reshape(B * S, PLE_DIM),
                    w_up.astype(BF16), row(b_up), w_down.astype(BF16), row(b_down),
                    w_pe.astype(BF16), w_pg.astype(BF16), row(ln2_g), row(ln2_b))
    return out.reshape(B, S, D)


def kernel(x, p, w_in, b_in, conv_dw_w, conv_dw_b, conv_ln_g, conv_ln_b, cmp_pos_k, cmp_w1_k, cmp_w2_k, cmp_pos_v, cmp_w1_v, cmp_w2_v, w_out, b_out, ln1_g, ln1_b, w_up, b_up, w_down, b_down, w_pe, w_pg, ln2_g, ln2_b):
    params = (w_in, b_in, conv_dw_w, conv_dw_b, conv_ln_g, conv_ln_b,
              cmp_pos_k, cmp_w1_k, cmp_w2_k, cmp_pos_v, cmp_w1_v, cmp_w2_v,
              w_out, b_out, ln1_g, ln1_b, w_up, b_up, w_down, b_down, w_pe, w_pg, ln2_g, ln2_b)
    for i in range(DEPTH):
        x = _layer(x, p[i], *[t[i] for t in params])
    return x
```

```python
import functools

import jax
import jax.numpy as jnp
import numpy as np
from jax import lax
from jax.experimental import pallas as pl
from jax.experimental.pallas import tpu as pltpu

F32 = jnp.float32
BF16 = jnp.bfloat16

D_MODEL = 1024
PLE_DIM = 256
CONV_CH = 512
CONV_WIDTH = 31
HEAD_DIM = 64
N_KV = 2
HPG = 4
N_BRANCH = 3
CMP_LEN = 32
CMP_STRIDE = 16
CMP_HIDDEN = 256
SLC_BLOCK = 64
SLC_TOPN = 16
WINDOW = 512
D_FF = 4 * D_MODEL
LN_EPS = 1e-5
NEG_INF = -1e30
FORCE_BONUS = 1e4
DEPTH = 1
ALPHA = (2 * DEPTH) ** 0.25

UNSELECTED_BIAS = -float(2 ** 30)
BELOW_NEG_INF = -3e38

SUBLANES = 8
PROJ_ROWS = 512
CONV_ROWS = 256
CONV_HALO = 32
CONV_ROW_CHUNK = 32
Q_TILE = 256
KEY_CHUNK = 256
WIN_UNIT = 128
CMP_UNIT = 128
LOG2_E = 1.4426950408889634
FFN_ROWS = 512
FFN_CHUNK = 1024
VMEM_LIMIT = 56 * 1024 * 1024

RM_COLS = 1792
VT_GROUP_ROWS = 80
VT_ROWS = N_KV * VT_GROUP_ROWS
T_ROWS = 512 + 2 * VT_ROWS + 32


def _layer_norm(z, g, b):
    mu = jnp.mean(z, axis=-1, keepdims=True)
    zc = z - mu
    var = jnp.mean(zc * zc, axis=-1, keepdims=True)
    return zc * lax.rsqrt(var + LN_EPS) * g + b


def _dot(a, b):
    return jnp.dot(a, b, preferred_element_type=F32)


def _dot_nt(a, b):
    return lax.dot_general(a, b, (((1,), (1,)), ((), ())), preferred_element_type=F32)


def _dot_tn(a, b):
    return lax.dot_general(a, b, (((0,), (0,)), ((), ())), preferred_element_type=F32)


def _proj_kernel(x_ref, wrm_ref, brm_ref, wt_ref, bt_ref,
                 vglu_ref, kc_ref, vc_ref, ks_ref, kw_ref,
                 qT_ref, vsT_ref, vwT_ref, gT_ref):
    xb = x_ref[0].astype(BF16)

    def rm(c0, c1):
        return _dot(xb, wrm_ref[:, c0:c1]) + brm_ref[:, c0:c1]

    half = CONV_CH // 2
    for c in range(0, CONV_CH, half):
        a = rm(c, c + half)
        g = rm(CONV_CH + c, CONV_CH + c + half)
        vglu_ref[0, :, c:c + half] = a * jax.nn.sigmoid(g)

    kv = rm(1024, 1280)
    kc_ref[0, 0] = kv[:, 0:64]
    kc_ref[0, 1] = kv[:, 64:128]
    vc_ref[0, 0] = kv[:, 128:192]
    vc_ref[0, 1] = kv[:, 192:256]

    kz = rm(1280, 1792).astype(BF16)
    ks_ref[0, 0] = kz[:, 0:128]
    ks_ref[0, 1] = kz[:, 128:256]
    kw_ref[0, 0] = kz[:, 256:384]
    kw_ref[0, 1] = kz[:, 384:512]

    def tr(r0, r1):
        return _dot_nt(wt_ref[r0:r1, :], xb) + bt_ref[r0:r1, :]

    qT_ref[0] = tr(0, 512).astype(BF16)
    r = 512
    vs = tr(r, r + VT_ROWS).astype(BF16)
    for u in range(PROJ_ROWS // KEY_CHUNK):
        vsT_ref[0, u] = vs[:, u * KEY_CHUNK:(u + 1) * KEY_CHUNK]
    r += VT_ROWS
    vw = tr(r, r + VT_ROWS).astype(BF16)
    for u in range(PROJ_ROWS // WIN_UNIT):
        vwT_ref[0, u] = vw[:, u * WIN_UNIT:(u + 1) * WIN_UNIT]
    r += VT_ROWS
    gT_ref[0] = jax.nn.sigmoid(tr(r, r + 32))


def _proj_call(x, wrm, brm, wt, bt):
    B, S, D = x.shape
    tm = PROJ_ROWS
    n = S // tm
    const = lambda b, i: (0, 0)
    out_shape = (
        jax.ShapeDtypeStruct((B, S, CONV_CH), F32),
        jax.ShapeDtypeStruct((B, N_KV, S, HEAD_DIM), F32),
        jax.ShapeDtypeStruct((B, N_KV, S, HEAD_DIM), F32),
        jax.ShapeDtypeStruct((B, N_KV, S, 128), BF16),
        jax.ShapeDtypeStruct((B, N_KV, S, 128), BF16),
        jax.ShapeDtypeStruct((B, 512, S), BF16),
        jax.ShapeDtypeStruct((B, S // KEY_CHUNK, VT_ROWS, KEY_CHUNK), BF16),
        jax.ShapeDtypeStruct((B, S // WIN_UNIT, VT_ROWS, WIN_UNIT), BF16),
        jax.ShapeDtypeStruct((B, 32, S), F32),
    )
    kvspec = pl.BlockSpec((1, N_KV, tm, HEAD_DIM), lambda b, i: (b, 0, i, 0))
    kzspec = pl.BlockSpec((1, N_KV, tm, 128), lambda b, i: (b, 0, i, 0))
    out_specs = (
        pl.BlockSpec((1, tm, CONV_CH), lambda b, i: (b, i, 0)),
        kvspec, kvspec, kzspec, kzspec,
        pl.BlockSpec((1, 512, tm), lambda b, i: (b, 0, i)),
        pl.BlockSpec((1, tm // KEY_CHUNK, VT_ROWS, KEY_CHUNK), lambda b, i: (b, i, 0, 0)),
        pl.BlockSpec((1, tm // WIN_UNIT, VT_ROWS, WIN_UNIT), lambda b, i: (b, i, 0, 0)),
        pl.BlockSpec((1, 32, tm), lambda b, i: (b, 0, i)),
    )
    in_specs = [
        pl.BlockSpec((1, tm, D), lambda b, i: (b, i, 0)),
        pl.BlockSpec((D, RM_COLS), const),
        pl.BlockSpec((1, RM_COLS), const),
        pl.BlockSpec((T_ROWS, D), const),
        pl.BlockSpec((T_ROWS, 1), const),
    ]
    return pl.pallas_call(
        _proj_kernel, grid=(B, n), in_specs=in_specs, out_specs=out_specs,
        out_shape=out_shape, name="proj",
        compiler_params=pltpu.CompilerParams(
            dimension_semantics=("parallel", "parallel"), vmem_limit_bytes=VMEM_LIMIT),
    )(x, wrm, brm, wt, bt)


def _conv_kernel(cur_ref, halo_ref, w_ref, b_ref, g_ref, beta_ref, o_ref, xs_ref):
    i = pl.program_id(1)
    n_shift = xs_ref.shape[1]
    xs_ref[0, 0:CONV_HALO, :] = jnp.where(i > 0, halo_ref[0], 0.0)
    xs_ref[0, CONV_HALO:, :] = cur_ref[0]
    x0 = xs_ref[0]
    for b in range(1, SUBLANES):
        xs_ref[b] = pltpu.roll(x0, n_shift - b, axis=0)
    lead = CONV_HALO - (CONV_WIDTH - 1)
    for r in range(0, CONV_ROWS, CONV_ROW_CHUNK):
        acc = jnp.zeros((CONV_ROW_CHUNK, CONV_CH), F32)
        for k in range(CONV_WIDTH):
            a, b = divmod(lead + k, SUBLANES)
            row = r + a * SUBLANES
            w_k = jnp.concatenate([w_ref[k]] * (CONV_ROW_CHUNK // SUBLANES), axis=0)
            acc = acc + xs_ref[b, row:row + CONV_ROW_CHUNK, :] * w_k
        y = _layer_norm(acc + b_ref[...], g_ref[...], beta_ref[...])
        o_ref[0, r:r + CONV_ROW_CHUNK, :] = (y * jax.nn.sigmoid(y)).astype(BF16)


def _conv_call(v, w, b, g, beta):
    B, S, C = v.shape
    tc = CONV_ROWS
    ratio = tc // CONV_HALO
    const = lambda b_, i: (0, 0)
    return pl.pallas_call(
        _conv_kernel, grid=(B, S // tc),
        in_specs=[
            pl.BlockSpec((1, tc, C), lambda b_, i: (b_, i, 0)),
            pl.BlockSpec((1, CONV_HALO, C), lambda b_, i: (b_, jnp.maximum(i * ratio - 1, 0), 0)),
            pl.BlockSpec((CONV_WIDTH, SUBLANES, C), lambda b_, i: (0, 0, 0)),
            pl.BlockSpec((1, C), const), pl.BlockSpec((1, C), const), pl.BlockSpec((1, C), const),
        ],
        out_specs=pl.BlockSpec((1, tc, C), lambda b_, i: (b_, i, 0)),
        out_shape=jax.ShapeDtypeStruct((B, S, C), BF16),
        scratch_shapes=[pltpu.VMEM((SUBLANES, tc + CONV_HALO, C), F32)],
        name="conv",
        compiler_params=pltpu.CompilerParams(
            dimension_semantics=("parallel", "parallel"), vmem_limit_bytes=VMEM_LIMIT),
    )(v, v, w, b, g, beta)


def _compress_hidden(r, pos_ref, w1_ref):
    half = CMP_STRIDE * HEAD_DIM
    n_rows = r.shape[0]
    a = _dot((r + pos_ref[0:1, :]).astype(BF16), w1_ref[0:half, :])
    b = _dot((r + pos_ref[1:2, :]).astype(BF16), w1_ref[half:2 * half, :])
    h = a + pltpu.roll(b, n_rows - 1, axis=0)
    return (h * jax.nn.sigmoid(h)).astype(BF16)


def _compress_kernel(rk_ref, rv_ref, pk_ref, w1k_ref, w2k_ref, pv_ref, w1v_ref, w2vT_ref,
                     kc_ref, vcT_ref):
    n_cmp = kc_ref.shape[2]

    def stride_rows(ref):
        return jnp.concatenate(
            [ref[0, 0, pl.ds(l, n_cmp, stride=CMP_STRIDE), :] for l in range(CMP_STRIDE)], axis=1)

    hk = _compress_hidden(stride_rows(rk_ref), pk_ref, w1k_ref)
    kc_ref[0, 0] = _dot(hk, w2k_ref[...]).astype(BF16)
    hv = _compress_hidden(stride_rows(rv_ref), pv_ref, w1v_ref)
    vcT_ref[0, 0] = _dot_nt(w2vT_ref[...], hv).astype(BF16)


def _compress_call(rk, rv, pk, w1k, w2k, pv, w1v, w2vT):
    B, G, S, _ = rk.shape
    NC = S // CMP_STRIDE
    W = CMP_STRIDE * HEAD_DIM
    const = lambda b, g: (0, 0)
    rspec = pl.BlockSpec((1, 1, S, HEAD_DIM), lambda b, g: (b, g, 0, 0))
    return pl.pallas_call(
        _compress_kernel, grid=(B, G),
        in_specs=[
            rspec, rspec,
            pl.BlockSpec((2, W), const), pl.BlockSpec((2 * W, CMP_HIDDEN), const),
            pl.BlockSpec((CMP_HIDDEN, 128), const),
            pl.BlockSpec((2, W), const), pl.BlockSpec((2 * W, CMP_HIDDEN), const),
            pl.BlockSpec((HEAD_DIM, CMP_HIDDEN), const),
        ],
        out_specs=(
            pl.BlockSpec((1, 1, NC, 128), lambda b, g: (b, g, 0, 0)),
            pl.BlockSpec((1, 1, HEAD_DIM, NC), lambda b, g: (b, g, 0, 0)),
        ),
        out_shape=(
            jax.ShapeDtypeStruct((B, G, NC, 128), BF16),
            jax.ShapeDtypeStruct((B, G, HEAD_DIM, NC), BF16),
        ),
        name="compress",
        compiler_params=pltpu.CompilerParams(
            dimension_semantics=("parallel", "parallel"), vmem_limit_bytes=VMEM_LIMIT),
    )(rk, rv, pk, w1k, w2k, pv, w1v, w2vT)


def _nsa_kernel(q_ref, g_ref, kc_ref, vcT_ref, ks_ref, vsT_ref, kw_ref, vwT_ref,
                e_ref, kdead_ref, mov_ref, cbias_ref, wbias_ref, tbias_ref,
                o_ref, q2_ref, s_a, s_b, p_a, p_b, acc_ref, oc_ref, imp_ref):
    T = Q_TILE
    C = KEY_CHUNK
    qb = pl.program_id(1)
    t0 = qb * T
    n_cmp = kc_ref.shape[2]
    n_slc = mov_ref.shape[0]
    n_chunks = vsT_ref.shape[1]
    groups = range(N_KV)
    heads = range(HPG)
    cols = [(g, h) for g in groups for h in heads]

    def col(g, h):
        return g * HPG + h

    q4 = q_ref[0]
    spare = jnp.where(lax.broadcasted_iota(jnp.int32, (HEAD_DIM, T), 0) == 0,
                      UNSELECTED_BIAS, 0.0).astype(BF16)
    q1 = [jnp.concatenate([q4[i * HEAD_DIM:(i + 1) * HEAD_DIM, :], spare], axis=0)
          for i in range(len(cols))]
    q1_all = [jnp.concatenate(q1[g * HPG:(g + 1) * HPG], axis=1) for g in groups]
    pos_t = t0 + lax.broadcasted_iota(jnp.int32, (1, T), 1)

    c_off = pl.multiple_of(n_cmp - qb * (T // CMP_STRIDE), T // CMP_STRIDE)

    def all_heads(a):
        return jnp.concatenate([a] * HPG, axis=1)

    def head_cols(a):
        return [a[:, h * T:(h + 1) * T] for h in heads]

    def compressed(rows):
        for g in groups:
            sc = (_dot(kc_ref[0, g, 0:rows, :], q1_all[g])
                  + all_heads(cbias_ref[pl.ds(c_off, rows), :]))
            m_c = jnp.max(sc, axis=0, keepdims=True)
            p_c = jnp.exp2(sc - m_c)
            l_c = jnp.sum(p_c, axis=0, keepdims=True)
            p_c = p_c * jnp.where(m_c > 0.5 * NEG_INF, 1.0 / l_c, 0.0)
            oc_ref[g] = _dot(vcT_ref[0, g, :, 0:rows], p_c.astype(BF16))
            psum = functools.reduce(lambda a, b: a + b, head_cols(p_c))
            mov = mov_ref[:, 0:rows]
            p_hi = psum.astype(BF16)
            rem = psum - p_hi.astype(F32)
            p_mid = rem.astype(BF16)
            p_lo = (rem - p_mid.astype(F32)).astype(BF16)
            imp_ref[g] = _dot(mov, p_hi) + _dot(mov, p_mid) + _dot(mov, p_lo)

    cmp_units = (t0 + T - CMP_LEN) // CMP_STRIDE // CMP_UNIT + 1
    for units in range(1, n_cmp // CMP_UNIT + 1):
        pl.when(cmp_units == units)(functools.partial(compressed, units * CMP_UNIT))

    w_keys = WINDOW + T
    start = pl.multiple_of(jnp.maximum(t0 - WINDOW, 0), WIN_UNIT)
    w_off = pl.multiple_of(WINDOW - (t0 - start), WIN_UNIT)
    u0 = start // WIN_UNIT
    o_w = []
    for g in groups:
        v_win = jnp.concatenate(
            [vwT_ref[0, u0 + u, g * VT_GROUP_ROWS:(g + 1) * VT_GROUP_ROWS, :]
             for u in range(w_keys // WIN_UNIT)], axis=1)
        sw = (_dot(kw_ref[0, g, pl.ds(start, w_keys), :], q1_all[g])
              + all_heads(wbias_ref[pl.ds(w_off, w_keys), :]))
        p_w = jnp.exp2(sw - jnp.max(sw, axis=0, keepdims=True)).astype(BF16)
        ow = _dot(v_win, p_w)
        o_w += head_cols(ow[0:HEAD_DIM] * (1.0 / ow[HEAD_DIM:HEAD_DIM + 1]))

    blk = lax.broadcasted_iota(jnp.int32, (n_slc, T), 0)
    blk_f = blk.astype(F32)
    cur = jnp.right_shift(pos_t, 6)
    forced = (blk == 0) | (blk == cur) | (blk == cur - 1)
    valid_b = (blk * SLC_BLOCK) <= pos_t
    bonus = jnp.where(forced, FORCE_BONUS, 0.0)
    work = [jnp.where(valid_b, imp_ref[g] + bonus, NEG_INF) for g in groups]
    for _ in range(SLC_TOPN):
        for g in groups:
            mx = jnp.max(work[g], axis=0, keepdims=True)
            first = jnp.min(jnp.where(work[g] == mx, blk_f, float(n_slc)), axis=0, keepdims=True)
            work[g] = jnp.where(blk_f == first, BELOW_NEG_INF, work[g])
    for g in groups:
        picked = work[g] < 0.5 * BELOW_NEG_INF
        unsel = jnp.where(picked & valid_b, 0.0, UNSELECTED_BIAS).astype(BF16)
        for h in heads:
            q2_ref[col(g, h)] = jnp.concatenate([q1[col(g, h)], unsel], axis=0)

    n_full = t0 // C
    uq = qb - n_full * (C // T)

    def key_operands(c, live):
        c = jnp.minimum(c, n_chunks - 1)
        off = pl.multiple_of(c * C, C)
        e_c = e_ref[pl.ds(off, C), :]
        out = []
        for g in groups:
            ks_c = ks_ref[0, g, pl.ds(off, C), :]
            if live is not None:
                ks_c = jnp.where(live, ks_c, kdead_ref[...])
            out.append(jnp.concatenate([ks_c, e_c], axis=1))
        return out

    def values(c):
        return [vsT_ref[0, c, g * VT_GROUP_ROWS:(g + 1) * VT_GROUP_ROWS, :] for g in groups]

    def scores(i, k2, sbuf, bias=None):
        s = _dot(k2, q2_ref[i])
        if bias is not None:
            s = s + bias
        sbuf[i] = s
        return jnp.max(s, axis=0, keepdims=True)

    def weights(i, sbuf, pbuf, cm, m_old):
        m_new = jnp.maximum(m_old, cm)
        pbuf[i] = jnp.exp2(sbuf[i] - m_new).astype(BF16)
        return m_new, jnp.exp2(m_old - m_new)

    def add_values(i, vt, pbuf, rescale):
        acc_ref[i] = rescale * acc_ref[i] + _dot(vt, pbuf[i])

    k2_t = key_operands(n_full, None)
    k2_0 = key_operands(0, 0 < n_full)
    t_off = pl.multiple_of((C // T - 1 - uq) * T, T)
    m0, cm0 = [], []
    for g, h in cols:
        i = col(g, h)
        cm_t = scores(i, k2_t[g], s_a, tbias_ref[pl.ds(t_off, C), :])
        cm0.append(scores(i, k2_0[g], s_b))
        acc_ref[i] = jnp.zeros((VT_GROUP_ROWS, T), F32)
        m0.append(weights(i, s_a, p_a, cm_t, cm_t)[0])

    def pair_body(trip, carry):
        m, scale_pend, c_pend, cm_b = carry
        c0 = 2 * trip
        k2_a = key_operands(c0 + 1, c0 + 1 < n_full)
        k2_b = key_operands(c0 + 2, c0 + 2 < n_full)
        vt_pend = values(c_pend)
        vt_0 = values(c0)
        m1, scale0, cm_a = [], [], []
        for g, h in cols:
            i = col(g, h)
            add_values(i, vt_pend[g], p_a, scale_pend[i])
            cm_a.append(scores(i, k2_a[g], s_a))
            m_i, s_i = weights(i, s_b, p_b, cm_b[i], m[i])
            m1.append(m_i)
            scale0.append(s_i)
        m2, scale1, cm_b2 = [], [], []
        for g, h in cols:
            i = col(g, h)
            cm_b2.append(scores(i, k2_b[g], s_b))
            add_values(i, vt_0[g], p_b, scale0[i])
            m_i, s_i = weights(i, s_a, p_a, cm_a[i], m1[i])
            m2.append(m_i)
            scale1.append(s_i)
        return tuple(m2), tuple(scale1), jnp.minimum(c0 + 1, n_chunks - 1), tuple(cm_b2)

    ones = tuple(jnp.ones((1, T), F32) for _ in cols)
    _, scale_pend, c_pend, _ = lax.fori_loop(
        0, (n_full + 1) // 2, pair_body, (tuple(m0), ones, n_full, tuple(cm0)))

    vt_pend = values(c_pend)
    for g, h in cols:
        i = col(g, h)
        add_values(i, vt_pend[g], p_a, scale_pend[i])
        acc = acc_ref[i]
        o_s = acc[0:HEAD_DIM] * (1.0 / acc[HEAD_DIM:HEAD_DIM + 1])
        o_c = oc_ref[g, :, h * T:(h + 1) * T]
        gate = [g_ref[0, pl.ds(i * N_BRANCH + br, 1), :] for br in range(N_BRANCH)]
        out = gate[0] * o_c + gate[1] * o_s + gate[2] * o_w[i]
        o_ref[0, i * HEAD_DIM:(i + 1) * HEAD_DIM, :] = out.astype(BF16)


def _nsa_call(qT, gT, kc, vcT, ks, vsT, kw, vwT, consts):
    B, _, S = qT.shape
    G = N_KV
    T = Q_TILE
    C = KEY_CHUNK
    n_cmp = kc.shape[2]
    rows = G * HPG * HEAD_DIM
    n_cols = G * HPG
    e_mat, kdead, movT, cbias, wbias, tbias = consts
    once = pl.Buffered(1)
    per_batch = lambda shape: pl.BlockSpec((1,) + shape, lambda b, i: (b, 0, 0, 0), pipeline_mode=once)
    full = lambda a: pl.BlockSpec(a.shape, lambda b, i: (0, 0), pipeline_mode=once)
    in_specs = [
        pl.BlockSpec((1, rows, T), lambda b, i: (b, 0, i)),
        pl.BlockSpec((1, 32, T), lambda b, i: (b, 0, i)),
        per_batch((G, n_cmp, 128)),
        per_batch((G, HEAD_DIM, n_cmp)),
        per_batch((G, S, 128)),
        per_batch((S // C, VT_ROWS, C)),
        per_batch((G, S, 128)),
        per_batch((S // WIN_UNIT, VT_ROWS, WIN_UNIT)),
        full(e_mat), full(kdead), full(movT), full(cbias), full(wbias), full(tbias),
    ]
    return pl.pallas_call(
        _nsa_kernel, grid=(B, S // T), in_specs=in_specs,
        out_specs=pl.BlockSpec((1, rows, T), lambda b, i: (b, 0, i)),
        out_shape=jax.ShapeDtypeStruct((B, rows, S), BF16),
        scratch_shapes=[pltpu.VMEM((n_cols, 256, T), BF16),
                        pltpu.VMEM((n_cols, C, T), F32), pltpu.VMEM((n_cols, C, T), F32),
                        pltpu.VMEM((n_cols, C, T), BF16), pltpu.VMEM((n_cols, C, T), BF16),
                        pltpu.VMEM((n_cols, VT_GROUP_ROWS, T), F32),
                        pltpu.VMEM((G, HEAD_DIM, HPG * T), F32),
                        pltpu.VMEM((G, S // SLC_BLOCK, T), F32)],
        name="nsa",
        compiler_params=pltpu.CompilerParams(
            dimension_semantics=("parallel", "arbitrary"),
            vmem_limit_bytes=VMEM_LIMIT),
    )(qT, gT, kc, vcT, ks, vsT, kw, vwT, e_mat, kdead, movT, cbias, wbias, tbias)


def _mix_ffn_kernel(yc_ref, ynT_ref, x_ref, p_ref, wc_ref, wn_ref, bo_ref, g1_ref, beta1_ref,
                    wup_ref, bup_ref, wdn_ref, bdn_ref, wpe_ref, wpg_ref, g2_ref, beta2_ref,
                    o_ref, acc_ref):
    mix = _dot(yc_ref[0], wc_ref[...]) + _dot_tn(ynT_ref[0], wn_ref[...])
    x1 = _layer_norm(ALPHA * x_ref[0] + mix + bo_ref[...], g1_ref[...], beta1_ref[...])
    xb = x1.astype(BF16)
    ple = _dot(p_ref[0].astype(BF16), wpe_ref[...]) * jax.nn.sigmoid(_dot(xb, wpg_ref[...]))
    acc_ref[...] = ALPHA * x1 + ple + bdn_ref[...]
    for c in range(0, D_FF, FFN_CHUNK):
        u = _dot(xb, wup_ref[:, c:c + FFN_CHUNK]) + bup_ref[:, c:c + FFN_CHUNK]
        u = jnp.square(jnp.maximum(u, 0.0)).astype(BF16)
        acc_ref[...] += _dot(u, wdn_ref[c:c + FFN_CHUNK, :])
    o_ref[0] = _layer_norm(acc_ref[...], g2_ref[...], beta2_ref[...])


def _mix_ffn_call(yc, ynT, x, p, wc, wn, bo, g1, beta1, wup, bup, wdn, bdn, wpe, wpg, g2, beta2):
    B, S, D = x.shape
    tm = FFN_ROWS
    const = lambda b, i: (0, 0)
    rows = lambda width: pl.BlockSpec((1, tm, width), lambda b, i: (b, i, 0))
    weight = lambda shape: pl.BlockSpec(shape, const, pipeline_mode=pl.Buffered(1))
    vec = pl.BlockSpec((1, D), const)
    return pl.pallas_call(
        _mix_ffn_kernel, grid=(B, S // tm),
        in_specs=[
            rows(CONV_CH),
            pl.BlockSpec((1, D - CONV_CH, tm), lambda b, i: (b, 0, i)),
            rows(D), rows(PLE_DIM),
            weight((CONV_CH, D)), weight((D - CONV_CH, D)), vec, vec, vec,
            weight((D, D_FF)), pl.BlockSpec((1, D_FF), const), weight((D_FF, D)), vec,
            weight((PLE_DIM, D)), weight((D, D)), vec, vec,
        ],
        out_specs=rows(D),
        out_shape=jax.ShapeDtypeStruct((B, S, D), F32),
        scratch_shapes=[pltpu.VMEM((tm, D), F32)],
        name="mixffn",
        compiler_params=pltpu.CompilerParams(
            dimension_semantics=("parallel", "parallel"), vmem_limit_bytes=VMEM_LIMIT),
    )(yc, ynT, x, p, wc, wn, bo, g1, beta1, wup, bup, wdn, bdn, wpe, wpg, g2, beta2)


def _pad_groups(w):
    z = jnp.zeros(w.shape[:-1] + (HEAD_DIM,), w.dtype)
    return jnp.concatenate([w[..., :HEAD_DIM], z, w[..., HEAD_DIM:], z], axis=-1)


def _overlap_matrix_t(n_cmp_padded, n_slc):
    n_cmp = n_cmp_padded - 1
    c0 = np.arange(n_cmp) * CMP_STRIDE
    c1 = c0 + CMP_LEN - 1
    s0 = np.arange(n_slc) * SLC_BLOCK
    s1 = s0 + SLC_BLOCK - 1
    m = ((c0[:, None] <= s1[None, :]) & (c1[:, None] >= s0[None, :])).astype(np.float32)
    out = np.zeros((n_slc, n_cmp_padded), np.float32)
    out[:, :n_cmp] = m.T
    return out


def _vt_rows(w, bias):
    d = w.shape[0]
    pad = VT_GROUP_ROWS - HEAD_DIM
    one = jnp.zeros((pad,), F32).at[0].set(1.0)
    ws, bs = [], []
    for g in range(N_KV):
        ws += [w[:, g * HEAD_DIM:(g + 1) * HEAD_DIM], jnp.zeros((d, pad), F32)]
        bs += [bias[g * HEAD_DIM:(g + 1) * HEAD_DIM], one]
    return jnp.concatenate(ws, axis=1), jnp.concatenate(bs)


def _nsa_constants(S, n_cmp_p):
    T, C = Q_TILE, KEY_CHUNK
    t = np.arange(T)[None, :]
    neg = np.float32(NEG_INF)

    def bias(valid):
        return jnp.asarray(np.where(valid, np.float32(0.0), neg))

    e_mat = (np.arange(S)[:, None] // SLC_BLOCK == np.arange(S // SLC_BLOCK)[None, :])
    kdead = np.zeros((C, 128), np.float32)
    kdead[:, HEAD_DIM] = 1.0
    j = np.arange(2 * n_cmp_p)[:, None]
    cbias = bias(CMP_STRIDE * (j - n_cmp_p) + CMP_LEN - 1 <= t)
    j = np.arange(2 * WINDOW + T)[:, None]
    wbias = bias((t < j) & (j <= WINDOW + t))
    j = np.arange(2 * C - T)[:, None]
    tbias = bias(j - (C - T) <= t)
    return (jnp.asarray(e_mat.astype(np.float32), BF16), jnp.asarray(kdead, BF16),
            jnp.asarray(_overlap_matrix_t(n_cmp_p, S // SLC_BLOCK), BF16), cbias, wbias, tbias)


def _layer(x, p, w_in, b_in, conv_dw_w, conv_dw_b, conv_ln_g, conv_ln_b,
           cmp_pos_k, cmp_w1_k, cmp_w2_k, cmp_pos_v, cmp_w1_v, cmp_w2_v,
           w_out, b_out, ln1_g, ln1_b, w_up, b_up, w_down, b_down, w_pe, w_pg, ln2_g, ln2_b):
    B, S, D = x.shape
    scale = HEAD_DIM ** -0.5 * LOG2_E
    row = lambda v: v.reshape(1, -1).astype(F32)

    w, bias = w_in, b_in
    wrm = jnp.concatenate(
        [w[:, 0:1024], w[:, 1536:1792], _pad_groups(w[:, 1792:1920]), _pad_groups(w[:, 2048:2176])],
        axis=1).astype(BF16)
    brm = row(jnp.concatenate(
        [bias[0:1024], bias[1536:1792], _pad_groups(bias[1792:1920]), _pad_groups(bias[2048:2176])]))
    w_vs, b_vs = _vt_rows(w[:, 1920:2048], bias[1920:2048])
    w_vw, b_vw = _vt_rows(w[:, 2176:2304], bias[2176:2304])
    wt = jnp.concatenate(
        [w[:, 1024:1536] * scale, w_vs, w_vw, w[:, 2304:2328], jnp.zeros((D, 8), F32)],
        axis=1).T.astype(BF16)
    bt = jnp.concatenate(
        [bias[1024:1536] * scale, b_vs, b_vw, bias[2304:2328], jnp.zeros((8,), F32)]).reshape(-1, 1)

    vglu, kc, vc, ks, kw, qT, vsT, vwT, gT = _proj_call(x, wrm, brm, wt, bt)

    conv_w = jnp.broadcast_to(conv_dw_w.reshape(CONV_WIDTH, 1, CONV_CH),
                              (CONV_WIDTH, SUBLANES, CONV_CH))
    y_conv = _conv_call(vglu, conv_w, row(conv_dw_b), row(conv_ln_g), row(conv_ln_b))

    n_cmp_p = S // CMP_STRIDE
    blk_w = CMP_STRIDE * HEAD_DIM
    w2k = jnp.concatenate([cmp_w2_k, jnp.zeros((CMP_HIDDEN, 128 - HEAD_DIM), F32)], axis=1)
    kcz, vcT = _compress_call(
        kc, vc, cmp_pos_k.reshape(2, blk_w), cmp_w1_k.astype(BF16), w2k.astype(BF16),
        cmp_pos_v.reshape(2, blk_w), cmp_w1_v.astype(BF16), cmp_w2_v.T.astype(BF16))

    y_nsaT = _nsa_call(qT, gT, kcz, vcT, ks, vsT, kw, vwT, _nsa_constants(S, n_cmp_p))

    wo = w_out.astype(BF16)
    return _mix_ffn_call(
        y_conv, y_nsaT, x, p, wo[:CONV_CH], wo[CONV_CH:], row(b_out), row(ln1_g), row(ln1_b),
        w_up.astype(BF16), row(b_up), w_down.astype(BF16), row(b_down),
        w_pe.astype(BF16), w_pg.astype(BF16), row(ln2_g), row(ln2_b))


def kernel(x, p, w_in, b_in, conv_dw_w, conv_dw_b, conv_ln_g, conv_ln_b, cmp_pos_k, cmp_w1_k, cmp_w2_k, cmp_pos_v, cmp_w1_v, cmp_w2_v, w_out, b_out, ln1_g, ln1_b, w_up, b_up, w_down, b_down, w_pe, w_pg, ln2_g, ln2_b):
    params = (w_in, b_in, conv_dw_w, conv_dw_b, conv_ln_g, conv_ln_b,
              cmp_pos_k, cmp_w1_k, cmp_w2_k, cmp_pos_v, cmp_w1_v, cmp_w2_v,
              w_out, b_out, ln1_g, ln1_b, w_up, b_up, w_down, b_down, w_pe, w_pg, ln2_g, ln2_b)
    for i in range(DEPTH):
        x = _layer(x, p[i], *[t[i] for t in params])
    return x
```

```python
import functools

import jax
import jax.numpy as jnp
import numpy as np
from jax import lax
from jax.experimental import pallas as pl
from jax.experimental.pallas import tpu as pltpu

F32 = jnp.float32
BF16 = jnp.bfloat16

D_MODEL = 1024
PLE_DIM = 256
CONV_CH = 512
CONV_WIDTH = 31
HEAD_DIM = 64
N_KV = 2
HPG = 4
N_BRANCH = 3
CMP_LEN = 32
CMP_STRIDE = 16
CMP_HIDDEN = 256
SLC_BLOCK = 64
SLC_TOPN = 16
WINDOW = 512
D_FF = 4 * D_MODEL
LN_EPS = 1e-5
NEG_INF = -1e30
FORCE_BONUS = 1e4
DEPTH = 1
ALPHA = (2 * DEPTH) ** 0.25

UNSELECTED_BIAS = -float(2 ** 30)
BELOW_NEG_INF = -3e38

SUBLANES = 8
PROJ_ROWS = 512
CONV_HALO = 32
CONV_ROW_CHUNK = 32
Q_TILE = 256
KEY_CHUNK = 256
WIN_UNIT = 128
CMP_UNIT = 128
LOG2_E = 1.4426950408889634
FFN_ROWS = 512
FFN_CHUNK = 1024
VMEM_LIMIT = 56 * 1024 * 1024

RM_COLS = 1792
VT_GROUP_ROWS = 80
VT_ROWS = N_KV * VT_GROUP_ROWS
T_ROWS = 512 + 2 * VT_ROWS + 32


def _layer_norm(z, g, b):
    mu = jnp.mean(z, axis=-1, keepdims=True)
    zc = z - mu
    var = jnp.mean(zc * zc, axis=-1, keepdims=True)
    return zc * lax.rsqrt(var + LN_EPS) * g + b


def _dot(a, b):
    return jnp.dot(a, b, preferred_element_type=F32)


def _dot_nt(a, b):
    return lax.dot_general(a, b, (((1,), (1,)), ((), ())), preferred_element_type=F32)


def _dot_tn(a, b):
    return lax.dot_general(a, b, (((0,), (0,)), ((), ())), preferred_element_type=F32)


def _causal_conv(xs_ref, w_ref, b_ref, g_ref, beta_ref, o_ref):
    n_shift = xs_ref.shape[1]
    x0 = xs_ref[0]
    for b in range(1, SUBLANES):
        xs_ref[b] = pltpu.roll(x0, n_shift - b, axis=0)
    lead = CONV_HALO - (CONV_WIDTH - 1)
    for r in range(0, n_shift - CONV_HALO, CONV_ROW_CHUNK):
        acc = jnp.zeros((CONV_ROW_CHUNK, CONV_CH), F32)
        for k in range(CONV_WIDTH):
            a, b = divmod(lead + k, SUBLANES)
            row = r + a * SUBLANES
            w_k = jnp.concatenate([w_ref[k]] * (CONV_ROW_CHUNK // SUBLANES), axis=0)
            acc = acc + xs_ref[b, row:row + CONV_ROW_CHUNK, :] * w_k
        y = _layer_norm(acc + b_ref[...], g_ref[...], beta_ref[...])
        o_ref[0, r:r + CONV_ROW_CHUNK, :] = (y * jax.nn.sigmoid(y)).astype(BF16)


def _proj_kernel(x_ref, wrm_ref, brm_ref, wt_ref, bt_ref, cw_ref, cb_ref, cg_ref, cbeta_ref,
                 yc_ref, kc_ref, vc_ref, ks_ref, kw_ref,
                 qT_ref, vsT_ref, vwT_ref, gT_ref, xs_ref, carry_ref):
    xb = x_ref[0].astype(BF16)

    def rm(c0, c1):
        return _dot(xb, wrm_ref[:, c0:c1]) + brm_ref[:, c0:c1]

    first_tile = pl.program_id(1) == 0
    xs_ref[0, 0:CONV_HALO, :] = jnp.where(first_tile, 0.0, carry_ref[...])
    half = CONV_CH // 2
    for c in range(0, CONV_CH, half):
        a = rm(c, c + half)
        g = rm(CONV_CH + c, CONV_CH + c + half)
        xs_ref[0, CONV_HALO:, c:c + half] = a * jax.nn.sigmoid(g)
    carry_ref[...] = xs_ref[0, PROJ_ROWS:PROJ_ROWS + CONV_HALO, :]
    _causal_conv(xs_ref, cw_ref, cb_ref, cg_ref, cbeta_ref, yc_ref)

    kv = rm(1024, 1280)
    kc_ref[0, 0] = kv[:, 0:64]
    kc_ref[0, 1] = kv[:, 64:128]
    vc_ref[0, 0] = kv[:, 128:192]
    vc_ref[0, 1] = kv[:, 192:256]

    kz = rm(1280, 1792).astype(BF16)
    ks_ref[0, 0] = kz[:, 0:128]
    ks_ref[0, 1] = kz[:, 128:256]
    kw_ref[0, 0] = kz[:, 256:384]
    kw_ref[0, 1] = kz[:, 384:512]

    def tr(r0, r1):
        return _dot_nt(wt_ref[r0:r1, :], xb) + bt_ref[r0:r1, :]

    qT_ref[0] = tr(0, 512).astype(BF16)
    r = 512
    vs = tr(r, r + VT_ROWS).astype(BF16)
    for u in range(PROJ_ROWS // KEY_CHUNK):
        vsT_ref[0, u] = vs[:, u * KEY_CHUNK:(u + 1) * KEY_CHUNK]
    r += VT_ROWS
    vw = tr(r, r + VT_ROWS).astype(BF16)
    for u in range(PROJ_ROWS // WIN_UNIT):
        vwT_ref[0, u] = vw[:, u * WIN_UNIT:(u + 1) * WIN_UNIT]
    r += VT_ROWS
    gT_ref[0] = jax.nn.sigmoid(tr(r, r + 32))


def _proj_call(x, wrm, brm, wt, bt, conv_w, conv_b, conv_g, conv_beta):
    B, S, D = x.shape
    tm = PROJ_ROWS
    n = S // tm
    const = lambda b, i: (0, 0)
    out_shape = (
        jax.ShapeDtypeStruct((B, S, CONV_CH), BF16),
        jax.ShapeDtypeStruct((B, N_KV, S, HEAD_DIM), F32),
        jax.ShapeDtypeStruct((B, N_KV, S, HEAD_DIM), F32),
        jax.ShapeDtypeStruct((B, N_KV, S, 128), BF16),
        jax.ShapeDtypeStruct((B, N_KV, S, 128), BF16),
        jax.ShapeDtypeStruct((B, 512, S), BF16),
        jax.ShapeDtypeStruct((B, S // KEY_CHUNK, VT_ROWS, KEY_CHUNK), BF16),
        jax.ShapeDtypeStruct((B, S // WIN_UNIT, VT_ROWS, WIN_UNIT), BF16),
        jax.ShapeDtypeStruct((B, 32, S), F32),
    )
    kvspec = pl.BlockSpec((1, N_KV, tm, HEAD_DIM), lambda b, i: (b, 0, i, 0))
    kzspec = pl.BlockSpec((1, N_KV, tm, 128), lambda b, i: (b, 0, i, 0))
    out_specs = (
        pl.BlockSpec((1, tm, CONV_CH), lambda b, i: (b, i, 0)),
        kvspec, kvspec, kzspec, kzspec,
        pl.BlockSpec((1, 512, tm), lambda b, i: (b, 0, i)),
        pl.BlockSpec((1, tm // KEY_CHUNK, VT_ROWS, KEY_CHUNK), lambda b, i: (b, i, 0, 0)),
        pl.BlockSpec((1, tm // WIN_UNIT, VT_ROWS, WIN_UNIT), lambda b, i: (b, i, 0, 0)),
        pl.BlockSpec((1, 32, tm), lambda b, i: (b, 0, i)),
    )
    in_specs = [
        pl.BlockSpec((1, tm, D), lambda b, i: (b, i, 0)),
        pl.BlockSpec((D, RM_COLS), const),
        pl.BlockSpec((1, RM_COLS), const),
        pl.BlockSpec((T_ROWS, D), const),
        pl.BlockSpec((T_ROWS, 1), const),
        pl.BlockSpec((CONV_WIDTH, SUBLANES, CONV_CH), lambda b, i: (0, 0, 0)),
        pl.BlockSpec((1, CONV_CH), const), pl.BlockSpec((1, CONV_CH), const),
        pl.BlockSpec((1, CONV_CH), const),
    ]
    return pl.pallas_call(
        _proj_kernel, grid=(B, n), in_specs=in_specs, out_specs=out_specs,
        out_shape=out_shape, name="proj",
        compiler_params=pltpu.CompilerParams(
            dimension_semantics=("parallel", "arbitrary"), vmem_limit_bytes=VMEM_LIMIT),
        scratch_shapes=[pltpu.VMEM((SUBLANES, tm + CONV_HALO, CONV_CH), F32),
                        pltpu.VMEM((CONV_HALO, CONV_CH), F32)],
    )(x, wrm, brm, wt, bt, conv_w, conv_b, conv_g, conv_beta)


def _compress_hidden(r, pos_ref, w1_ref):
    half = CMP_STRIDE * HEAD_DIM
    n_rows = r.shape[0]
    a = _dot((r + pos_ref[0:1, :]).astype(BF16), w1_ref[0:half, :])
    b = _dot((r + pos_ref[1:2, :]).astype(BF16), w1_ref[half:2 * half, :])
    h = a + pltpu.roll(b, n_rows - 1, axis=0)
    return (h * jax.nn.sigmoid(h)).astype(BF16)


def _compress_kernel(rk_ref, rv_ref, pk_ref, w1k_ref, w2k_ref, pv_ref, w1v_ref, w2vT_ref,
                     kc_ref, vcT_ref):
    n_cmp = kc_ref.shape[2]

    def stride_rows(ref):
        return jnp.concatenate(
            [ref[0, 0, pl.ds(l, n_cmp, stride=CMP_STRIDE), :] for l in range(CMP_STRIDE)], axis=1)

    hk = _compress_hidden(stride_rows(rk_ref), pk_ref, w1k_ref)
    kc_ref[0, 0] = _dot(hk, w2k_ref[...]).astype(BF16)
    hv = _compress_hidden(stride_rows(rv_ref), pv_ref, w1v_ref)
    vcT_ref[0, 0] = _dot_nt(w2vT_ref[...], hv).astype(BF16)


def _compress_call(rk, rv, pk, w1k, w2k, pv, w1v, w2vT):
    B, G, S, _ = rk.shape
    NC = S // CMP_STRIDE
    W = CMP_STRIDE * HEAD_DIM
    const = lambda b, g: (0, 0)
    rspec = pl.BlockSpec((1, 1, S, HEAD_DIM), lambda b, g: (b, g, 0, 0))
    return pl.pallas_call(
        _compress_kernel, grid=(B, G),
        in_specs=[
            rspec, rspec,
            pl.BlockSpec((2, W), const), pl.BlockSpec((2 * W, CMP_HIDDEN), const),
            pl.BlockSpec((CMP_HIDDEN, 128), const),
            pl.BlockSpec((2, W), const), pl.BlockSpec((2 * W, CMP_HIDDEN), const),
            pl.BlockSpec((HEAD_DIM, CMP_HIDDEN), const),
        ],
        out_specs=(
            pl.BlockSpec((1, 1, NC, 128), lambda b, g: (b, g, 0, 0)),
            pl.BlockSpec((1, 1, HEAD_DIM, NC), lambda b, g: (b, g, 0, 0)),
        ),
        out_shape=(
            jax.ShapeDtypeStruct((B, G, NC, 128), BF16),
            jax.ShapeDtypeStruct((B, G, HEAD_DIM, NC), BF16),
        ),
        name="compress",
        compiler_params=pltpu.CompilerParams(
            dimension_semantics=("parallel", "parallel"), vmem_limit_bytes=VMEM_LIMIT),
    )(rk, rv, pk, w1k, w2k, pv, w1v, w2vT)


def _nsa_kernel(q_ref, g_ref, kc_ref, vcT_ref, ks_ref, vsT_ref, kw_ref, vwT_ref,
                e_ref, kdead_ref, mov_ref, cbias_ref, wbias_ref, tbias_ref,
                o_ref, q2_ref, s_a, s_b, p_a, p_b, acc_ref, oc_ref, imp_ref):
    T = Q_TILE
    C = KEY_CHUNK
    qb = pl.program_id(1)
    t0 = qb * T
    n_cmp = kc_ref.shape[2]
    n_slc = mov_ref.shape[0]
    n_chunks = vsT_ref.shape[1]
    groups = range(N_KV)
    heads = range(HPG)
    cols = [(g, h) for g in groups for h in heads]

    def col(g, h):
        return g * HPG + h

    q4 = q_ref[0]
    spare = jnp.where(lax.broadcasted_iota(jnp.int32, (HEAD_DIM, T), 0) == 0,
                      UNSELECTED_BIAS, 0.0).astype(BF16)
    q1 = [jnp.concatenate([q4[i * HEAD_DIM:(i + 1) * HEAD_DIM, :], spare], axis=0)
          for i in range(len(cols))]
    q1_all = [jnp.concatenate(q1[g * HPG:(g + 1) * HPG], axis=1) for g in groups]
    pos_t = t0 + lax.broadcasted_iota(jnp.int32, (1, T), 1)

    c_off = pl.multiple_of(n_cmp - qb * (T // CMP_STRIDE), T // CMP_STRIDE)

    def all_heads(a):
        return jnp.concatenate([a] * HPG, axis=1)

    def head_cols(a):
        return [a[:, h * T:(h + 1) * T] for h in heads]

    def compressed(rows):
        for g in groups:
            sc = (_dot(kc_ref[0, g, 0:rows, :], q1_all[g])
                  + all_heads(cbias_ref[pl.ds(c_off, rows), :]))
            m_c = jnp.max(sc, axis=0, keepdims=True)
            p_c = jnp.exp2(sc - m_c)
            l_c = jnp.sum(p_c, axis=0, keepdims=True)
            p_c = p_c * jnp.where(m_c > 0.5 * NEG_INF, 1.0 / l_c, 0.0)
            oc_ref[g] = _dot(vcT_ref[0, g, :, 0:rows], p_c.astype(BF16))
            psum = functools.reduce(lambda a, b: a + b, head_cols(p_c))
            mov = mov_ref[:, 0:rows]
            p_hi = psum.astype(BF16)
            rem = psum - p_hi.astype(F32)
            p_mid = rem.astype(BF16)
            p_lo = (rem - p_mid.astype(F32)).astype(BF16)
            imp_ref[g] = _dot(mov, p_hi) + _dot(mov, p_mid) + _dot(mov, p_lo)

    cmp_units = (t0 + T - CMP_LEN) // CMP_STRIDE // CMP_UNIT + 1
    for units in range(1, n_cmp // CMP_UNIT + 1):
        pl.when(cmp_units == units)(functools.partial(compressed, units * CMP_UNIT))

    w_keys = WINDOW + T
    start = pl.multiple_of(jnp.maximum(t0 - WINDOW, 0), WIN_UNIT)
    w_off = pl.multiple_of(WINDOW - (t0 - start), WIN_UNIT)
    u0 = start // WIN_UNIT
    o_w = []
    for g in groups:
        v_win = jnp.concatenate(
            [vwT_ref[0, u0 + u, g * VT_GROUP_ROWS:(g + 1) * VT_GROUP_ROWS, :]
             for u in range(w_keys // WIN_UNIT)], axis=1)
        sw = (_dot(kw_ref[0, g, pl.ds(start, w_keys), :], q1_all[g])
              + all_heads(wbias_ref[pl.ds(w_off, w_keys), :]))
        p_w = jnp.exp2(sw - jnp.max(sw, axis=0, keepdims=True)).astype(BF16)
        ow = _dot(v_win, p_w)
        o_w += head_cols(ow[0:HEAD_DIM] * (1.0 / ow[HEAD_DIM:HEAD_DIM + 1]))

    blk = lax.broadcasted_iota(jnp.int32, (n_slc, T), 0)
    blk_f = blk.astype(F32)
    cur = jnp.right_shift(pos_t, 6)
    forced = (blk == 0) | (blk == cur) | (blk == cur - 1)
    valid_b = (blk * SLC_BLOCK) <= pos_t
    bonus = jnp.where(forced, FORCE_BONUS, 0.0)
    work = [jnp.where(valid_b, imp_ref[g] + bonus, NEG_INF) for g in groups]
    for _ in range(SLC_TOPN):
        for g in groups:
            mx = jnp.max(work[g], axis=0, keepdims=True)
            first = jnp.min(jnp.where(work[g] == mx, blk_f, float(n_slc)), axis=0, keepdims=True)
            work[g] = jnp.where(blk_f == first, BELOW_NEG_INF, work[g])
    for g in groups:
        picked = work[g] < 0.5 * BELOW_NEG_INF
        unsel = jnp.where(picked & valid_b, 0.0, UNSELECTED_BIAS).astype(BF16)
        for h in heads:
            q2_ref[col(g, h)] = jnp.concatenate([q1[col(g, h)], unsel], axis=0)

    n_full = t0 // C
    uq = qb - n_full * (C // T)

    def key_operands(c, live):
        c = jnp.minimum(c, n_chunks - 1)
        off = pl.multiple_of(c * C, C)
        e_c = e_ref[pl.ds(off, C), :]
        out = []
        for g in groups:
            ks_c = ks_ref[0, g, pl.ds(off, C), :]
            if live is not None:
                ks_c = jnp.where(live, ks_c, kdead_ref[...])
            out.append(jnp.concatenate([ks_c, e_c], axis=1))
        return out

    def values(c):
        return [vsT_ref[0, c, g * VT_GROUP_ROWS:(g + 1) * VT_GROUP_ROWS, :] for g in groups]

    def scores(i, k2, sbuf, bias=None):
        s = _dot(k2, q2_ref[i])
        if bias is not None:
            s = s + bias
        sbuf[i] = s
        return jnp.max(s, axis=0, keepdims=True)

    def weights(i, sbuf, pbuf, cm, m_old):
        m_new = jnp.maximum(m_old, cm)
        pbuf[i] = jnp.exp2(sbuf[i] - m_new).astype(BF16)
        return m_new, jnp.exp2(m_old - m_new)

    def add_values(i, vt, pbuf, rescale):
        acc_ref[i] = rescale * acc_ref[i] + _dot(vt, pbuf[i])

    k2_t = key_operands(n_full, None)
    k2_0 = key_operands(0, 0 < n_full)
    t_off = pl.multiple_of((C // T - 1 - uq) * T, T)
    m0, cm0 = [], []
    for g, h in cols:
        i = col(g, h)
        cm_t = scores(i, k2_t[g], s_a, tbias_ref[pl.ds(t_off, C), :])
        cm0.append(scores(i, k2_0[g], s_b))
        acc_ref[i] = jnp.zeros((VT_GROUP_ROWS, T), F32)
        m0.append(weights(i, s_a, p_a, cm_t, cm_t)[0])

    def pair_body(trip, carry):
        m, scale_pend, c_pend, cm_b = carry
        c0 = 2 * trip
        k2_a = key_operands(c0 + 1, c0 + 1 < n_full)
        k2_b = key_operands(c0 + 2, c0 + 2 < n_full)
        vt_pend = values(c_pend)
        vt_0 = values(c0)
        m1, scale0, cm_a = [], [], []
        for g, h in cols:
            i = col(g, h)
            add_values(i, vt_pend[g], p_a, scale_pend[i])
            cm_a.append(scores(i, k2_a[g], s_a))
            m_i, s_i = weights(i, s_b, p_b, cm_b[i], m[i])
            m1.append(m_i)
            scale0.append(s_i)
        m2, scale1, cm_b2 = [], [], []
        for g, h in cols:
            i = col(g, h)
            cm_b2.append(scores(i, k2_b[g], s_b))
            add_values(i, vt_0[g], p_b, scale0[i])
            m_i, s_i = weights(i, s_a, p_a, cm_a[i], m1[i])
            m2.append(m_i)
            scale1.append(s_i)
        return tuple(m2), tuple(scale1), jnp.minimum(c0 + 1, n_chunks - 1), tuple(cm_b2)

    ones = tuple(jnp.ones((1, T), F32) for _ in cols)
    _, scale_pend, c_pend, _ = lax.fori_loop(
        0, (n_full + 1) // 2, pair_body, (tuple(m0), ones, n_full, tuple(cm0)))

    vt_pend = values(c_pend)
    for g, h in cols:
        i = col(g, h)
        add_values(i, vt_pend[g], p_a, scale_pend[i])
        acc = acc_ref[i]
        o_s = acc[0:HEAD_DIM] * (1.0 / acc[HEAD_DIM:HEAD_DIM + 1])
        o_c = oc_ref[g, :, h * T:(h + 1) * T]
        gate = [g_ref[0, pl.ds(i * N_BRANCH + br, 1), :] for br in range(N_BRANCH)]
        out = gate[0] * o_c + gate[1] * o_s + gate[2] * o_w[i]
        o_ref[0, i * HEAD_DIM:(i + 1) * HEAD_DIM, :] = out.astype(BF16)


def _nsa_call(qT, gT, kc, vcT, ks, vsT, kw, vwT, consts):
    B, _, S = qT.shape
    G = N_KV
    T = Q_TILE
    C = KEY_CHUNK
    n_cmp = kc.shape[2]
    rows = G * HPG * HEAD_DIM
    n_cols = G * HPG
    e_mat, kdead, movT, cbias, wbias, tbias = consts
    once = pl.Buffered(1)
    per_batch = lambda shape: pl.BlockSpec((1,) + shape, lambda b, i: (b, 0, 0, 0), pipeline_mode=once)
    full = lambda a: pl.BlockSpec(a.shape, lambda b, i: (0, 0), pipeline_mode=once)
    in_specs = [
        pl.BlockSpec((1, rows, T), lambda b, i: (b, 0, i)),
        pl.BlockSpec((1, 32, T), lambda b, i: (b, 0, i)),
        per_batch((G, n_cmp, 128)),
        per_batch((G, HEAD_DIM, n_cmp)),
        per_batch((G, S, 128)),
        per_batch((S // C, VT_ROWS, C)),
        per_batch((G, S, 128)),
        per_batch((S // WIN_UNIT, VT_ROWS, WIN_UNIT)),
        full(e_mat), full(kdead), full(movT), full(cbias), full(wbias), full(tbias),
    ]
    return pl.pallas_call(
        _nsa_kernel, grid=(B, S // T), in_specs=in_specs,
        out_specs=pl.BlockSpec((1, rows, T), lambda b, i: (b, 0, i)),
        out_shape=jax.ShapeDtypeStruct((B, rows, S), BF16),
        scratch_shapes=[pltpu.VMEM((n_cols, 256, T), BF16),
                        pltpu.VMEM((n_cols, C, T), F32), pltpu.VMEM((n_cols, C, T), F32),
                        pltpu.VMEM((n_cols, C, T), BF16), pltpu.VMEM((n_cols, C, T), BF16),
                        pltpu.VMEM((n_cols, VT_GROUP_ROWS, T), F32),
                        pltpu.VMEM((G, HEAD_DIM, HPG * T), F32),
                        pltpu.VMEM((G, S // SLC_BLOCK, T), F32)],
        name="nsa",
        compiler_params=pltpu.CompilerParams(
            dimension_semantics=("parallel", "arbitrary"),
            vmem_limit_bytes=VMEM_LIMIT),
    )(qT, gT, kc, vcT, ks, vsT, kw, vwT, e_mat, kdead, movT, cbias, wbias, tbias)


def _mix_ffn_kernel(yc_ref, ynT_ref, x_ref, p_ref, wc_ref, wn_ref, bo_ref, g1_ref, beta1_ref,
                    wup_ref, bup_ref, wdn_ref, bdn_ref, wpe_ref, wpg_ref, g2_ref, beta2_ref,
                    o_ref, acc_ref):
    mix = _dot(yc_ref[0], wc_ref[...]) + _dot_tn(ynT_ref[0], wn_ref[...])
    x1 = _layer_norm(ALPHA * x_ref[0] + mix + bo_ref[...], g1_ref[...], beta1_ref[...])
    xb = x1.astype(BF16)
    ple = _dot(p_ref[0].astype(BF16), wpe_ref[...]) * jax.nn.sigmoid(_dot(xb, wpg_ref[...]))
    acc_ref[...] = ALPHA * x1 + ple + bdn_ref[...]
    for c in range(0, D_FF, FFN_CHUNK):
        u = _dot(xb, wup_ref[:, c:c + FFN_CHUNK]) + bup_ref[:, c:c + FFN_CHUNK]
        u = jnp.square(jnp.maximum(u, 0.0)).astype(BF16)
        acc_ref[...] += _dot(u, wdn_ref[c:c + FFN_CHUNK, :])
    o_ref[0] = _layer_norm(acc_ref[...], g2_ref[...], beta2_ref[...])


def _mix_ffn_call(yc, ynT, x, p, wc, wn, bo, g1, beta1, wup, bup, wdn, bdn, wpe, wpg, g2, beta2):
    B, S, D = x.shape
    tm = FFN_ROWS
    const = lambda b, i: (0, 0)
    rows = lambda width: pl.BlockSpec((1, tm, width), lambda b, i: (b, i, 0))
    weight = lambda shape: pl.BlockSpec(shape, const, pipeline_mode=pl.Buffered(1))
    vec = pl.BlockSpec((1, D), const)
    return pl.pallas_call(
        _mix_ffn_kernel, grid=(B, S // tm),
        in_specs=[
            rows(CONV_CH),
            pl.BlockSpec((1, D - CONV_CH, tm), lambda b, i: (b, 0, i)),
            rows(D), rows(PLE_DIM),
            weight((CONV_CH, D)), weight((D - CONV_CH, D)), vec, vec, vec,
            weight((D, D_FF)), pl.BlockSpec((1, D_FF), const), weight((D_FF, D)), vec,
            weight((PLE_DIM, D)), weight((D, D)), vec, vec,
        ],
        out_specs=rows(D),
        out_shape=jax.ShapeDtypeStruct((B, S, D), F32),
        scratch_shapes=[pltpu.VMEM((tm, D), F32)],
        name="mixffn",
        compiler_params=pltpu.CompilerParams(
            dimension_semantics=("parallel", "parallel"), vmem_limit_bytes=VMEM_LIMIT),
    )(yc, ynT, x, p, wc, wn, bo, g1, beta1, wup, bup, wdn, bdn, wpe, wpg, g2, beta2)


def _pad_groups(w):
    z = jnp.zeros(w.shape[:-1] + (HEAD_DIM,), w.dtype)
    return jnp.concatenate([w[..., :HEAD_DIM], z, w[..., HEAD_DIM:], z], axis=-1)


def _overlap_matrix_t(n_cmp_padded, n_slc):
    n_cmp = n_cmp_padded - 1
    c0 = np.arange(n_cmp) * CMP_STRIDE
    c1 = c0 + CMP_LEN - 1
    s0 = np.arange(n_slc) * SLC_BLOCK
    s1 = s0 + SLC_BLOCK - 1
    m = ((c0[:, None] <= s1[None, :]) & (c1[:, None] >= s0[None, :])).astype(np.float32)
    out = np.zeros((n_slc, n_cmp_padded), np.float32)
    out[:, :n_cmp] = m.T
    return out


def _vt_rows(w, bias):
    d = w.shape[0]
    pad = VT_GROUP_ROWS - HEAD_DIM
    one = jnp.zeros((pad,), F32).at[0].set(1.0)
    ws, bs = [], []
    for g in range(N_KV):
        ws += [w[:, g * HEAD_DIM:(g + 1) * HEAD_DIM], jnp.zeros((d, pad), F32)]
        bs += [bias[g * HEAD_DIM:(g + 1) * HEAD_DIM], one]
    return jnp.concatenate(ws, axis=1), jnp.concatenate(bs)


def _nsa_constants(S, n_cmp_p):
    T, C = Q_TILE, KEY_CHUNK
    t = np.arange(T)[None, :]
    neg = np.float32(NEG_INF)

    def bias(valid):
        return jnp.asarray(np.where(valid, np.float32(0.0), neg))

    e_mat = (np.arange(S)[:, None] // SLC_BLOCK == np.arange(S // SLC_BLOCK)[None, :])
    kdead = np.zeros((C, 128), np.float32)
    kdead[:, HEAD_DIM] = 1.0
    j = np.arange(2 * n_cmp_p)[:, None]
    cbias = bias(CMP_STRIDE * (j - n_cmp_p) + CMP_LEN - 1 <= t)
    j = np.arange(2 * WINDOW + T)[:, None]
    wbias = bias((t < j) & (j <= WINDOW + t))
    j = np.arange(2 * C - T)[:, None]
    tbias = bias(j - (C - T) <= t)
    return (jnp.asarray(e_mat.astype(np.float32), BF16), jnp.asarray(kdead, BF16),
            jnp.asarray(_overlap_matrix_t(n_cmp_p, S // SLC_BLOCK), BF16), cbias, wbias, tbias)


def _layer(x, p, w_in, b_in, conv_dw_w, conv_dw_b, conv_ln_g, conv_ln_b,
           cmp_pos_k, cmp_w1_k, cmp_w2_k, cmp_pos_v, cmp_w1_v, cmp_w2_v,
           w_out, b_out, ln1_g, ln1_b, w_up, b_up, w_down, b_down, w_pe, w_pg, ln2_g, ln2_b):
    B, S, D = x.shape
    scale = HEAD_DIM ** -0.5 * LOG2_E
    row = lambda v: v.reshape(1, -1).astype(F32)

    w, bias = w_in, b_in
    wrm = jnp.concatenate(
        [w[:, 0:1024], w[:, 1536:1792], _pad_groups(w[:, 1792:1920]), _pad_groups(w[:, 2048:2176])],
        axis=1).astype(BF16)
    brm = row(jnp.concatenate(
        [bias[0:1024], bias[1536:1792], _pad_groups(bias[1792:1920]), _pad_groups(bias[2048:2176])]))
    w_vs, b_vs = _vt_rows(w[:, 1920:2048], bias[1920:2048])
    w_vw, b_vw = _vt_rows(w[:, 2176:2304], bias[2176:2304])
    wt = jnp.concatenate(
        [w[:, 1024:1536] * scale, w_vs, w_vw, w[:, 2304:2328], jnp.zeros((D, 8), F32)],
        axis=1).T.astype(BF16)
    bt = jnp.concatenate(
        [bias[1024:1536] * scale, b_vs, b_vw, bias[2304:2328], jnp.zeros((8,), F32)]).reshape(-1, 1)

    conv_w = jnp.broadcast_to(conv_dw_w.reshape(CONV_WIDTH, 1, CONV_CH),
                              (CONV_WIDTH, SUBLANES, CONV_CH))
    y_conv, kc, vc, ks, kw, qT, vsT, vwT, gT = _proj_call(
        x, wrm, brm, wt, bt, conv_w, row(conv_dw_b), row(conv_ln_g), row(conv_ln_b))

    n_cmp_p = S // CMP_STRIDE
    blk_w = CMP_STRIDE * HEAD_DIM
    w2k = jnp.concatenate([cmp_w2_k, jnp.zeros((CMP_HIDDEN, 128 - HEAD_DIM), F32)], axis=1)
    kcz, vcT = _compress_call(
        kc, vc, cmp_pos_k.reshape(2, blk_w), cmp_w1_k.astype(BF16), w2k.astype(BF16),
        cmp_pos_v.reshape(2, blk_w), cmp_w1_v.astype(BF16), cmp_w2_v.T.astype(BF16))

    y_nsaT = _nsa_call(qT, gT, kcz, vcT, ks, vsT, kw, vwT, _nsa_constants(S, n_cmp_p))

    wo = w_out.astype(BF16)
    return _mix_ffn_call(
        y_conv, y_nsaT, x, p, wo[:CONV_CH], wo[CONV_CH:], row(b_out), row(ln1_g), row(ln1_b),
        w_up.astype(BF16), row(b_up), w_down.astype(BF16), row(b_down),
        w_pe.astype(BF16), w_pg.astype(BF16), row(ln2_g), row(ln2_b))


def kernel(x, p, w_in, b_in, conv_dw_w, conv_dw_b, conv_ln_g, conv_ln_b, cmp_pos_k, cmp_w1_k, cmp_w2_k, cmp_pos_v, cmp_w1_v, cmp_w2_v, w_out, b_out, ln1_g, ln1_b, w_up, b_up, w_down, b_down, w_pe, w_pg, ln2_g, ln2_b):
    params = (w_in, b_in, conv_dw_w, conv_dw_b, conv_ln_g, conv_ln_b,
              cmp_pos_k, cmp_w1_k, cmp_w2_k, cmp_pos_v, cmp_w1_v, cmp_w2_v,
              w_out, b_out, ln1_g, ln1_b, w_up, b_up, w_down, b_down, w_pe, w_pg, ln2_g, ln2_b)
    for i in range(DEPTH):
        x = _layer(x, p[i], *[t[i] for t in params])
    return x
```

```python
import functools

import jax
import jax.numpy as jnp
import numpy as np
from jax import lax
from jax.experimental import pallas as pl
from jax.experimental.pallas import tpu as pltpu

F32 = jnp.float32
BF16 = jnp.bfloat16

D_MODEL = 1024
PLE_DIM = 256
CONV_CH = 512
CONV_WIDTH = 31
HEAD_DIM = 64
N_KV = 2
HPG = 4
N_BRANCH = 3
CMP_LEN = 32
CMP_STRIDE = 16
CMP_HIDDEN = 256
SLC_BLOCK = 64
SLC_TOPN = 16
WINDOW = 512
D_FF = 4 * D_MODEL
LN_EPS = 1e-5
NEG_INF = -1e30
FORCE_BONUS = 1e4
DEPTH = 1
ALPHA = (2 * DEPTH) ** 0.25

UNSELECTED_BIAS = -float(2 ** 30)
BELOW_NEG_INF = -3e38

SUBLANES = 8
PROJ_ROWS = 512
CONV_HALO = 32
CONV_ROW_CHUNK = 32
Q_TILE = 256
KEY_CHUNK = 256
WIN_UNIT = 128
CMP_UNIT = 128
LOG2_E = 1.4426950408889634
FFN_ROWS = 512
FFN_CHUNK = 1024
VMEM_LIMIT = 56 * 1024 * 1024

RM_COLS = 1792
VT_GROUP_ROWS = 80
VT_ROWS = N_KV * VT_GROUP_ROWS
T_ROWS = 512 + 2 * VT_ROWS + 32


def _layer_norm(z, g, b):
    mu = jnp.mean(z, axis=-1, keepdims=True)
    zc = z - mu
    var = jnp.mean(zc * zc, axis=-1, keepdims=True)
    return zc * lax.rsqrt(var + LN_EPS) * g + b


def _dot(a, b):
    return jnp.dot(a, b, preferred_element_type=F32)


def _dot_nt(a, b):
    return lax.dot_general(a, b, (((1,), (1,)), ((), ())), preferred_element_type=F32)


def _dot_tn(a, b):
    return lax.dot_general(a, b, (((0,), (0,)), ((), ())), preferred_element_type=F32)


def _causal_conv(xs_ref, w_ref, b_ref, g_ref, beta_ref, o_ref, side_jobs):
    side_jobs = list(side_jobs)
    launched = []
    n_shift = xs_ref.shape[1]
    n_chunks = (n_shift - CONV_HALO) // CONV_ROW_CHUNK
    every = max(1, n_chunks // max(1, len(side_jobs)))
    x0 = xs_ref[0]
    for b in range(1, SUBLANES):
        xs_ref[b] = pltpu.roll(x0, n_shift - b, axis=0)
    lead = CONV_HALO - (CONV_WIDTH - 1)
    for r in range(0, n_shift - CONV_HALO, CONV_ROW_CHUNK):
        acc = jnp.zeros((CONV_ROW_CHUNK, CONV_CH), F32)
        if (r // CONV_ROW_CHUNK) % every == every - 1:
            if launched:
                bits = pltpu.bitcast(launched.pop(0), jnp.uint32)
                zero = pltpu.bitcast((bits >> 16) >> 16, F32)
                acc = acc + jnp.tile(zero, (CONV_ROW_CHUNK // SUBLANES, CONV_CH // 128))
            if side_jobs:
                launched.append(side_jobs.pop(0)())
        for k in range(CONV_WIDTH):
            a, b = divmod(lead + k, SUBLANES)
            row = r + a * SUBLANES
            w_k = jnp.concatenate([w_ref[k]] * (CONV_ROW_CHUNK // SUBLANES), axis=0)
            acc = acc + xs_ref[b, row:row + CONV_ROW_CHUNK, :] * w_k
        y = _layer_norm(acc + b_ref[...], g_ref[...], beta_ref[...])
        o_ref[0, r:r + CONV_ROW_CHUNK, :] = (y * jax.nn.sigmoid(y)).astype(BF16)
    for job in side_jobs:
        job()


def _proj_kernel(x_ref, wrm_ref, brm_ref, wt_ref, bt_ref, cw_ref, cb_ref, cg_ref, cbeta_ref,
                 yc_ref, kc_ref, vc_ref, ks_ref, kw_ref,
                 qT_ref, vsT_ref, vwT_ref, gT_ref, xs_ref, carry_ref):
    xb = x_ref[0].astype(BF16)

    def rm(c0, c1):
        return _dot(xb, wrm_ref[:, c0:c1]) + brm_ref[:, c0:c1]

    first_tile = pl.program_id(1) == 0
    xs_ref[0, 0:CONV_HALO, :] = jnp.where(first_tile, 0.0, carry_ref[...])
    half = CONV_CH // 2
    for c in range(0, CONV_CH, half):
        a = rm(c, c + half)
        g = rm(CONV_CH + c, CONV_CH + c + half)
        xs_ref[0, CONV_HALO:, c:c + half] = a * jax.nn.sigmoid(g)
    carry_ref[...] = xs_ref[0, PROJ_ROWS:PROJ_ROWS + CONV_HALO, :]

    def tr(r0, r1):
        return _dot_nt(wt_ref[r0:r1, :], xb) + bt_ref[r0:r1, :]

    def tile(v):
        return v[v.shape[0] - SUBLANES:, v.shape[1] - 128:]

    def compress_inputs():
        kv = rm(1024, 1280)
        kc_ref[0, 0] = kv[:, 0:64]
        kc_ref[0, 1] = kv[:, 64:128]
        vc_ref[0, 0] = kv[:, 128:192]
        vc_ref[0, 1] = kv[:, 192:256]
        return tile(kv)

    def keys(c0, ref):
        kz = rm(c0, c0 + 256)
        ref[0, 0] = kz[:, 0:128].astype(BF16)
        ref[0, 1] = kz[:, 128:256].astype(BF16)
        return tile(kz)

    def queries(r0):
        q = tr(r0, r0 + 256)
        qT_ref[0, r0:r0 + 256, :] = q.astype(BF16)
        return tile(q)

    def values(r0, ref, unit):
        v = tr(r0, r0 + VT_ROWS)
        for u in range(PROJ_ROWS // unit):
            ref[0, u] = v[:, u * unit:(u + 1) * unit].astype(BF16)
        return tile(v)

    def gates():
        r0 = 512 + 2 * VT_ROWS
        gt = tr(r0, r0 + 32)
        gT_ref[0] = jax.nn.sigmoid(gt)
        return tile(gt)

    P = functools.partial
    _causal_conv(xs_ref, cw_ref, cb_ref, cg_ref, cbeta_ref, yc_ref, [
        compress_inputs, P(keys, 1280, ks_ref), P(keys, 1536, kw_ref), P(queries, 0),
        P(queries, 256), P(values, 512, vsT_ref, KEY_CHUNK),
        P(values, 512 + VT_ROWS, vwT_ref, WIN_UNIT), gates])


def _proj_call(x, wrm, brm, wt, bt, conv_w, conv_b, conv_g, conv_beta):
    B, S, D = x.shape
    tm = PROJ_ROWS
    n = S // tm
    const = lambda b, i: (0, 0)
    out_shape = (
        jax.ShapeDtypeStruct((B, S, CONV_CH), BF16),
        jax.ShapeDtypeStruct((B, N_KV, S, HEAD_DIM), F32),
        jax.ShapeDtypeStruct((B, N_KV, S, HEAD_DIM), F32),
        jax.ShapeDtypeStruct((B, N_KV, S, 128), BF16),
        jax.ShapeDtypeStruct((B, N_KV, S, 128), BF16),
        jax.ShapeDtypeStruct((B, 512, S), BF16),
        jax.ShapeDtypeStruct((B, S // KEY_CHUNK, VT_ROWS, KEY_CHUNK), BF16),
        jax.ShapeDtypeStruct((B, S // WIN_UNIT, VT_ROWS, WIN_UNIT), BF16),
        jax.ShapeDtypeStruct((B, 32, S), F32),
    )
    kvspec = pl.BlockSpec((1, N_KV, tm, HEAD_DIM), lambda b, i: (b, 0, i, 0))
    kzspec = pl.BlockSpec((1, N_KV, tm, 128), lambda b, i: (b, 0, i, 0))
    out_specs = (
        pl.BlockSpec((1, tm, CONV_CH), lambda b, i: (b, i, 0)),
        kvspec, kvspec, kzspec, kzspec,
        pl.BlockSpec((1, 512, tm), lambda b, i: (b, 0, i)),
        pl.BlockSpec((1, tm // KEY_CHUNK, VT_ROWS, KEY_CHUNK), lambda b, i: (b, i, 0, 0)),
        pl.BlockSpec((1, tm // WIN_UNIT, VT_ROWS, WIN_UNIT), lambda b, i: (b, i, 0, 0)),
        pl.BlockSpec((1, 32, tm), lambda b, i: (b, 0, i)),
    )
    in_specs = [
        pl.BlockSpec((1, tm, D), lambda b, i: (b, i, 0)),
        pl.BlockSpec((D, RM_COLS), const),
        pl.BlockSpec((1, RM_COLS), const),
        pl.BlockSpec((T_ROWS, D), const),
        pl.BlockSpec((T_ROWS, 1), const),
        pl.BlockSpec((CONV_WIDTH, SUBLANES, CONV_CH), lambda b, i: (0, 0, 0)),
        pl.BlockSpec((1, CONV_CH), const), pl.BlockSpec((1, CONV_CH), const),
        pl.BlockSpec((1, CONV_CH), const),
    ]
    return pl.pallas_call(
        _proj_kernel, grid=(B, n), in_specs=in_specs, out_specs=out_specs,
        out_shape=out_shape, name="proj",
        compiler_params=pltpu.CompilerParams(
            dimension_semantics=("parallel", "arbitrary"), vmem_limit_bytes=VMEM_LIMIT),
        scratch_shapes=[pltpu.VMEM((SUBLANES, tm + CONV_HALO, CONV_CH), F32),
                        pltpu.VMEM((CONV_HALO, CONV_CH), F32)],
    )(x, wrm, brm, wt, bt, conv_w, conv_b, conv_g, conv_beta)


def _compress_hidden(r, pos_ref, w1_ref):
    half = CMP_STRIDE * HEAD_DIM
    n_rows = r.shape[0]
    a = _dot((r + pos_ref[0:1, :]).astype(BF16), w1_ref[0:half, :])
    b = _dot((r + pos_ref[1:2, :]).astype(BF16), w1_ref[half:2 * half, :])
    h = a + pltpu.roll(b, n_rows - 1, axis=0)
    return (h * jax.nn.sigmoid(h)).astype(BF16)


def _compress_kernel(rk_ref, rv_ref, pk_ref, w1k_ref, w2k_ref, pv_ref, w1v_ref, w2vT_ref,
                     kc_ref, vcT_ref):
    n_cmp = kc_ref.shape[2]

    def stride_rows(ref):
        return jnp.concatenate(
            [ref[0, 0, pl.ds(l, n_cmp, stride=CMP_STRIDE), :] for l in range(CMP_STRIDE)], axis=1)

    hk = _compress_hidden(stride_rows(rk_ref), pk_ref, w1k_ref)
    kc_ref[0, 0] = _dot(hk, w2k_ref[...]).astype(BF16)
    hv = _compress_hidden(stride_rows(rv_ref), pv_ref, w1v_ref)
    vcT_ref[0, 0] = _dot_nt(w2vT_ref[...], hv).astype(BF16)


def _compress_call(rk, rv, pk, w1k, w2k, pv, w1v, w2vT):
    B, G, S, _ = rk.shape
    NC = S // CMP_STRIDE
    W = CMP_STRIDE * HEAD_DIM
    const = lambda b, g: (0, 0)
    rspec = pl.BlockSpec((1, 1, S, HEAD_DIM), lambda b, g: (b, g, 0, 0))
    return pl.pallas_call(
        _compress_kernel, grid=(B, G),
        in_specs=[
            rspec, rspec,
            pl.BlockSpec((2, W), const), pl.BlockSpec((2 * W, CMP_HIDDEN), const),
            pl.BlockSpec((CMP_HIDDEN, 128), const),
            pl.BlockSpec((2, W), const), pl.BlockSpec((2 * W, CMP_HIDDEN), const),
            pl.BlockSpec((HEAD_DIM, CMP_HIDDEN), const),
        ],
        out_specs=(
            pl.BlockSpec((1, 1, NC, 128), lambda b, g: (b, g, 0, 0)),
            pl.BlockSpec((1, 1, HEAD_DIM, NC), lambda b, g: (b, g, 0, 0)),
        ),
        out_shape=(
            jax.ShapeDtypeStruct((B, G, NC, 128), BF16),
            jax.ShapeDtypeStruct((B, G, HEAD_DIM, NC), BF16),
        ),
        name="compress",
        compiler_params=pltpu.CompilerParams(
            dimension_semantics=("parallel", "parallel"), vmem_limit_bytes=VMEM_LIMIT),
    )(rk, rv, pk, w1k, w2k, pv, w1v, w2vT)


def _nsa_kernel(q_ref, g_ref, kc_ref, vcT_ref, ks_ref, vsT_ref, kw_ref, vwT_ref,
                e_ref, kdead_ref, mov_ref, cbias_ref, wbias_ref, tbias_ref,
                o_ref, q2_ref, s_a, s_b, p_a, p_b, acc_ref, oc_ref, imp_ref):
    T = Q_TILE
    C = KEY_CHUNK
    qb = pl.program_id(1)
    t0 = qb * T
    n_cmp = kc_ref.shape[2]
    n_slc = mov_ref.shape[0]
    n_chunks = vsT_ref.shape[1]
    groups = range(N_KV)
    heads = range(HPG)
    cols = [(g, h) for g in groups for h in heads]

    def col(g, h):
        return g * HPG + h

    q4 = q_ref[0]
    spare = jnp.where(lax.broadcasted_iota(jnp.int32, (HEAD_DIM, T), 0) == 0,
                      UNSELECTED_BIAS, 0.0).astype(BF16)
    q1 = [jnp.concatenate([q4[i * HEAD_DIM:(i + 1) * HEAD_DIM, :], spare], axis=0)
          for i in range(len(cols))]
    q1_all = [jnp.concatenate(q1[g * HPG:(g + 1) * HPG], axis=1) for g in groups]
    pos_t = t0 + lax.broadcasted_iota(jnp.int32, (1, T), 1)

    c_off = pl.multiple_of(n_cmp - qb * (T // CMP_STRIDE), T // CMP_STRIDE)

    def all_heads(a):
        return jnp.concatenate([a] * HPG, axis=1)

    def head_cols(a):
        return [a[:, h * T:(h + 1) * T] for h in heads]

    def compressed(rows):
        for g in groups:
            sc = (_dot(kc_ref[0, g, 0:rows, :], q1_all[g])
                  + all_heads(cbias_ref[pl.ds(c_off, rows), :]))
            m_c = jnp.max(sc, axis=0, keepdims=True)
            p_c = jnp.exp2(sc - m_c)
            l_c = jnp.sum(p_c, axis=0, keepdims=True)
            p_c = p_c * jnp.where(m_c > 0.5 * NEG_INF, 1.0 / l_c, 0.0)
            oc_ref[g] = _dot(vcT_ref[0, g, :, 0:rows], p_c.astype(BF16))
            psum = functools.reduce(lambda a, b: a + b, head_cols(p_c))
            mov = mov_ref[:, 0:rows]
            p_hi = psum.astype(BF16)
            rem = psum - p_hi.astype(F32)
            p_mid = rem.astype(BF16)
            p_lo = (rem - p_mid.astype(F32)).astype(BF16)
            imp_ref[g] = _dot(mov, p_hi) + _dot(mov, p_mid) + _dot(mov, p_lo)

    cmp_units = (t0 + T - CMP_LEN) // CMP_STRIDE // CMP_UNIT + 1
    for units in range(1, n_cmp // CMP_UNIT + 1):
        pl.when(cmp_units == units)(functools.partial(compressed, units * CMP_UNIT))

    w_keys = WINDOW + T
    start = pl.multiple_of(jnp.maximum(t0 - WINDOW, 0), WIN_UNIT)
    w_off = pl.multiple_of(WINDOW - (t0 - start), WIN_UNIT)
    u0 = start // WIN_UNIT
    sw_raw = [_dot(kw_ref[0, g, pl.ds(start, w_keys), :], q1_all[g]) for g in groups]

    def exact_zero(tile):
        bits = pltpu.bitcast(tile, jnp.uint32)
        return pltpu.bitcast((bits >> 16) >> 16, F32)

    blk = lax.broadcasted_iota(jnp.int32, (n_slc, T), 0)
    blk_f = blk.astype(F32)
    cur = jnp.right_shift(pos_t, 6)
    forced = (blk == 0) | (blk == cur) | (blk == cur - 1)
    valid_b = (blk * SLC_BLOCK) <= pos_t
    bonus = jnp.where(forced, FORCE_BONUS, 0.0)
    work = [jnp.where(valid_b, imp_ref[g] + bonus, NEG_INF) for g in groups]
    for rnd in range(SLC_TOPN):
        for g in groups:
            mx = jnp.max(work[g], axis=0, keepdims=True)
            if rnd == (g + 1) * SLC_TOPN // (N_KV + 1):
                mx = mx + jnp.tile(exact_zero(sw_raw[g][0:SUBLANES, 0:128])[0:1], (1, T // 128))
            first = jnp.min(jnp.where(work[g] == mx, blk_f, float(n_slc)), axis=0, keepdims=True)
            work[g] = jnp.where(blk_f == first, BELOW_NEG_INF, work[g])
    for g in groups:
        picked = work[g] < 0.5 * BELOW_NEG_INF
        unsel = jnp.where(picked & valid_b, 0.0, UNSELECTED_BIAS).astype(BF16)
        for h in heads:
            q2_ref[col(g, h)] = jnp.concatenate([q1[col(g, h)], unsel], axis=0)

    n_full = t0 // C
    uq = qb - n_full * (C // T)

    def key_operands(c, live):
        c = jnp.minimum(c, n_chunks - 1)
        off = pl.multiple_of(c * C, C)
        e_c = e_ref[pl.ds(off, C), :]
        out = []
        for g in groups:
            ks_c = ks_ref[0, g, pl.ds(off, C), :]
            if live is not None:
                ks_c = jnp.where(live, ks_c, kdead_ref[...])
            out.append(jnp.concatenate([ks_c, e_c], axis=1))
        return out

    def values(c):
        return [vsT_ref[0, c, g * VT_GROUP_ROWS:(g + 1) * VT_GROUP_ROWS, :] for g in groups]

    def scores(i, k2, sbuf, bias=None):
        s = _dot(k2, q2_ref[i])
        if bias is not None:
            s = s + bias
        sbuf[i] = s
        return jnp.max(s, axis=0, keepdims=True)

    def weights(i, sbuf, pbuf, cm, m_old):
        m_new = jnp.maximum(m_old, cm)
        pbuf[i] = jnp.exp2(sbuf[i] - m_new).astype(BF16)
        return m_new, jnp.exp2(m_old - m_new)

    def add_values(i, vt, pbuf, rescale):
        acc_ref[i] = rescale * acc_ref[i] + _dot(vt, pbuf[i])

    k2_t = key_operands(n_full, None)
    k2_0 = key_operands(0, 0 < n_full)
    t_off = pl.multiple_of((C // T - 1 - uq) * T, T)
    m0, cm0, cm_t = [], [], []
    for g, h in cols:
        i = col(g, h)
        cm_t.append(scores(i, k2_t[g], s_a, tbias_ref[pl.ds(t_off, C), :]))
        cm0.append(scores(i, k2_0[g], s_b))
        acc_ref[i] = jnp.zeros((VT_GROUP_ROWS, T), F32)

    scores_done = exact_zero(functools.reduce(lambda a, b: a + b, cm_t + cm0))
    o_w = []
    for g in groups:
        v_win = jnp.concatenate(
            [vwT_ref[0, u0 + u, g * VT_GROUP_ROWS:(g + 1) * VT_GROUP_ROWS, :]
             for u in range(w_keys // WIN_UNIT)], axis=1)
        sw = sw_raw[g] + all_heads(wbias_ref[pl.ds(w_off, w_keys), :])
        m_w = jnp.max(sw, axis=0, keepdims=True)
        if g == N_KV - 1:
            m_w = m_w + all_heads(scores_done)
        p_w = jnp.exp2(sw - m_w).astype(BF16)
        ow = _dot(v_win, p_w)
        o_w += head_cols(ow[0:HEAD_DIM] * (1.0 / ow[HEAD_DIM:HEAD_DIM + 1]))

    for i in range(len(cols)):
        m0.append(weights(i, s_a, p_a, cm_t[i], cm_t[i])[0])

    def pair_step(c0, carry):
        m, scale_pend, c_pend, cm_b = carry
        k2_a = key_operands(c0 + 1, c0 + 1 < n_full)
        k2_b = key_operands(c0 + 2, c0 + 2 < n_full)
        vt_pend = values(c_pend)
        vt_0 = values(c0)
        m1, scale0, cm_a = [], [], []
        for g, h in cols:
            i = col(g, h)
            add_values(i, vt_pend[g], p_a, scale_pend[i])
            cm_a.append(scores(i, k2_a[g], s_a))
            m_i, s_i = weights(i, s_b, p_b, cm_b[i], m[i])
            m1.append(m_i)
            scale0.append(s_i)
        m2, scale1, cm_b2 = [], [], []
        for g, h in cols:
            i = col(g, h)
            cm_b2.append(scores(i, k2_b[g], s_b))
            add_values(i, vt_0[g], p_b, scale0[i])
            m_i, s_i = weights(i, s_a, p_a, cm_a[i], m1[i])
            m2.append(m_i)
            scale1.append(s_i)
        return tuple(m2), tuple(scale1), jnp.minimum(c0 + 1, n_chunks - 1), tuple(cm_b2)

    ones = tuple(jnp.ones((1, T), F32) for _ in cols)
    _, scale_pend, c_pend, _ = lax.fori_loop(
        0, (n_full + 1) // 2, lambda t, c: pair_step(2 * t, c),
        (tuple(m0), ones, n_full, tuple(cm0)))

    vt_pend = values(c_pend)
    for g, h in cols:
        i = col(g, h)
        add_values(i, vt_pend[g], p_a, scale_pend[i])
        acc = acc_ref[i]
        o_s = acc[0:HEAD_DIM] * (1.0 / acc[HEAD_DIM:HEAD_DIM + 1])
        o_c = oc_ref[g, :, h * T:(h + 1) * T]
        gate = [g_ref[0, pl.ds(i * N_BRANCH + br, 1), :] for br in range(N_BRANCH)]
        out = gate[0] * o_c + gate[1] * o_s + gate[2] * o_w[i]
        o_ref[0, i * HEAD_DIM:(i + 1) * HEAD_DIM, :] = out.astype(BF16)


def _nsa_call(qT, gT, kc, vcT, ks, vsT, kw, vwT, consts):
    B, _, S = qT.shape
    G = N_KV
    T = Q_TILE
    C = KEY_CHUNK
    n_cmp = kc.shape[2]
    rows = G * HPG * HEAD_DIM
    n_cols = G * HPG
    e_mat, kdead, movT, cbias, wbias, tbias = consts
    once = pl.Buffered(1)
    per_batch = lambda shape: pl.BlockSpec((1,) + shape, lambda b, i: (b, 0, 0, 0), pipeline_mode=once)
    full = lambda a: pl.BlockSpec(a.shape, lambda b, i: (0, 0), pipeline_mode=once)
    in_specs = [
        pl.BlockSpec((1, rows, T), lambda b, i: (b, 0, i)),
        pl.BlockSpec((1, 32, T), lambda b, i: (b, 0, i)),
        per_batch((G, n_cmp, 128)),
        per_batch((G, HEAD_DIM, n_cmp)),
        per_batch((G, S, 128)),
        per_batch((S // C, VT_ROWS, C)),
        per_batch((G, S, 128)),
        per_batch((S // WIN_UNIT, VT_ROWS, WIN_UNIT)),
        full(e_mat), full(kdead), full(movT), full(cbias), full(wbias), full(tbias),
    ]
    return pl.pallas_call(
        _nsa_kernel, grid=(B, S // T), in_specs=in_specs,
        out_specs=pl.BlockSpec((1, rows, T), lambda b, i: (b, 0, i)),
        out_shape=jax.ShapeDtypeStruct((B, rows, S), BF16),
        scratch_shapes=[pltpu.VMEM((n_cols, 256, T), BF16),
                        pltpu.VMEM((n_cols, C, T), F32), pltpu.VMEM((n_cols, C, T), F32),
                        pltpu.VMEM((n_cols, C, T), BF16), pltpu.VMEM((n_cols, C, T), BF16),
                        pltpu.VMEM((n_cols, VT_GROUP_ROWS, T), F32),
                        pltpu.VMEM((G, HEAD_DIM, HPG * T), F32),
                        pltpu.VMEM((G, S // SLC_BLOCK, T), F32)],
        name="nsa",
        compiler_params=pltpu.CompilerParams(
            dimension_semantics=("parallel", "arbitrary"),
            vmem_limit_bytes=VMEM_LIMIT),
    )(qT, gT, kc, vcT, ks, vsT, kw, vwT, e_mat, kdead, movT, cbias, wbias, tbias)


def _mix_ffn_kernel(yc_ref, ynT_ref, x_ref, p_ref, wc_ref, wn_ref, bo_ref, g1_ref, beta1_ref,
                    wup_ref, bup_ref, wdn_ref, bdn_ref, wpe_ref, wpg_ref, g2_ref, beta2_ref,
                    o_ref, acc_ref):
    mix = _dot(yc_ref[0], wc_ref[...]) + _dot_tn(ynT_ref[0], wn_ref[...])
    x1 = _layer_norm(ALPHA * x_ref[0] + mix + bo_ref[...], g1_ref[...], beta1_ref[...])
    xb = x1.astype(BF16)
    ple = _dot(p_ref[0].astype(BF16), wpe_ref[...]) * jax.nn.sigmoid(_dot(xb, wpg_ref[...]))
    acc_ref[...] = ALPHA * x1 + ple + bdn_ref[...]
    for c in range(0, D_FF, FFN_CHUNK):
        u = _dot(xb, wup_ref[:, c:c + FFN_CHUNK]) + bup_ref[:, c:c + FFN_CHUNK]
        u = jnp.square(jnp.maximum(u, 0.0)).astype(BF16)
        acc_ref[...] += _dot(u, wdn_ref[c:c + FFN_CHUNK, :])
    o_ref[0] = _layer_norm(acc_ref[...], g2_ref[...], beta2_ref[...])


def _mix_ffn_call(yc, ynT, x, p, wc, wn, bo, g1, beta1, wup, bup, wdn, bdn, wpe, wpg, g2, beta2):
    B, S, D = x.shape
    tm = FFN_ROWS
    const = lambda b, i: (0, 0)
    rows = lambda width: pl.BlockSpec((1, tm, width), lambda b, i: (b, i, 0))
    weight = lambda shape: pl.BlockSpec(shape, const, pipeline_mode=pl.Buffered(1))
    vec = pl.BlockSpec((1, D), const)
    return pl.pallas_call(
        _mix_ffn_kernel, grid=(B, S // tm),
        in_specs=[
            rows(CONV_CH),
            pl.BlockSpec((1, D - CONV_CH, tm), lambda b, i: (b, 0, i)),
            rows(D), rows(PLE_DIM),
            weight((CONV_CH, D)), weight((D - CONV_CH, D)), vec, vec, vec,
            weight((D, D_FF)), pl.BlockSpec((1, D_FF), const), weight((D_FF, D)), vec,
            weight((PLE_DIM, D)), weight((D, D)), vec, vec,
        ],
        out_specs=rows(D),
        out_shape=jax.ShapeDtypeStruct((B, S, D), F32),
        scratch_shapes=[pltpu.VMEM((tm, D), F32)],
        name="mixffn",
        compiler_params=pltpu.CompilerParams(
            dimension_semantics=("parallel", "parallel"), vmem_limit_bytes=VMEM_LIMIT),
    )(yc, ynT, x, p, wc, wn, bo, g1, beta1, wup, bup, wdn, bdn, wpe, wpg, g2, beta2)


def _pad_groups(w):
    z = jnp.zeros(w.shape[:-1] + (HEAD_DIM,), w.dtype)
    return jnp.concatenate([w[..., :HEAD_DIM], z, w[..., HEAD_DIM:], z], axis=-1)


def _overlap_matrix_t(n_cmp_padded, n_slc):
    n_cmp = n_cmp_padded - 1
    c0 = np.arange(n_cmp) * CMP_STRIDE
    c1 = c0 + CMP_LEN - 1
    s0 = np.arange(n_slc) * SLC_BLOCK
    s1 = s0 + SLC_BLOCK - 1
    m = ((c0[:, None] <= s1[None, :]) & (c1[:, None] >= s0[None, :])).astype(np.float32)
    out = np.zeros((n_slc, n_cmp_padded), np.float32)
    out[:, :n_cmp] = m.T
    return out


def _vt_rows(w, bias):
    d = w.shape[0]
    pad = VT_GROUP_ROWS - HEAD_DIM
    one = jnp.zeros((pad,), F32).at[0].set(1.0)
    ws, bs = [], []
    for g in range(N_KV):
        ws += [w[:, g * HEAD_DIM:(g + 1) * HEAD_DIM], jnp.zeros((d, pad), F32)]
        bs += [bias[g * HEAD_DIM:(g + 1) * HEAD_DIM], one]
    return jnp.concatenate(ws, axis=1), jnp.concatenate(bs)


def _nsa_constants(S, n_cmp_p):
    T, C = Q_TILE, KEY_CHUNK
    t = np.arange(T)[None, :]
    neg = np.float32(NEG_INF)

    def bias(valid):
        return jnp.asarray(np.where(valid, np.float32(0.0), neg))

    e_mat = (np.arange(S)[:, None] // SLC_BLOCK == np.arange(S // SLC_BLOCK)[None, :])
    kdead = np.zeros((C, 128), np.float32)
    kdead[:, HEAD_DIM] = 1.0
    j = np.arange(2 * n_cmp_p)[:, None]
    cbias = bias(CMP_STRIDE * (j - n_cmp_p) + CMP_LEN - 1 <= t)
    j = np.arange(2 * WINDOW + T)[:, None]
    wbias = bias((t < j) & (j <= WINDOW + t))
    j = np.arange(2 * C - T)[:, None]
    tbias = bias(j - (C - T) <= t)
    return (jnp.asarray(e_mat.astype(np.float32), BF16), jnp.asarray(kdead, BF16),
            jnp.asarray(_overlap_matrix_t(n_cmp_p, S // SLC_BLOCK), BF16), cbias, wbias, tbias)


def _layer(x, p, w_in, b_in, conv_dw_w, conv_dw_b, conv_ln_g, conv_ln_b,
           cmp_pos_k, cmp_w1_k, cmp_w2_k, cmp_pos_v, cmp_w1_v, cmp_w2_v,
           w_out, b_out, ln1_g, ln1_b, w_up, b_up, w_down, b_down, w_pe, w_pg, ln2_g, ln2_b):
    B, S, D = x.shape
    scale = HEAD_DIM ** -0.5 * LOG2_E
    row = lambda v: v.reshape(1, -1).astype(F32)

    w, bias = w_in, b_in
    wrm = jnp.concatenate(
        [w[:, 0:1024], w[:, 1536:1792], _pad_groups(w[:, 1792:1920]), _pad_groups(w[:, 2048:2176])],
        axis=1).astype(BF16)
    brm = row(jnp.concatenate(
        [bias[0:1024], bias[1536:1792], _pad_groups(bias[1792:1920]), _pad_groups(bias[2048:2176])]))
    w_vs, b_vs = _vt_rows(w[:, 1920:2048], bias[1920:2048])
    w_vw, b_vw = _vt_rows(w[:, 2176:2304], bias[2176:2304])
    wt = jnp.concatenate(
        [w[:, 1024:1536] * scale, w_vs, w_vw, w[:, 2304:2328], jnp.zeros((D, 8), F32)],
        axis=1).T.astype(BF16)
    bt = jnp.concatenate(
        [bias[1024:1536] * scale, b_vs, b_vw, bias[2304:2328], jnp.zeros((8,), F32)]).reshape(-1, 1)

    conv_w = jnp.broadcast_to(conv_dw_w.reshape(CONV_WIDTH, 1, CONV_CH),
                              (CONV_WIDTH, SUBLANES, CONV_CH))
    y_conv, kc, vc, ks, kw, qT, vsT, vwT, gT = _proj_call(
        x, wrm, brm, wt, bt, conv_w, row(conv_dw_b), row(conv_ln_g), row(conv_ln_b))

    n_cmp_p = S // CMP_STRIDE
    blk_w = CMP_STRIDE * HEAD_DIM
    w2k = jnp.concatenate([cmp_w2_k, jnp.zeros((CMP_HIDDEN, 128 - HEAD_DIM), F32)], axis=1)
    kcz, vcT = _compress_call(
        kc, vc, cmp_pos_k.reshape(2, blk_w), cmp_w1_k.astype(BF16), w2k.astype(BF16),
        cmp_pos_v.reshape(2, blk_w), cmp_w1_v.astype(BF16), cmp_w2_v.T.astype(BF16))

    y_nsaT = _nsa_call(qT, gT, kcz, vcT, ks, vsT, kw, vwT, _nsa_constants(S, n_cmp_p))

    wo = w_out.astype(BF16)
    return _mix_ffn_call(
        y_conv, y_nsaT, x, p, wo[:CONV_CH], wo[CONV_CH:], row(b_out), row(ln1_g), row(ln1_b),
        w_up.astype(BF16), row(b_up), w_down.astype(BF16), row(b_down),
        w_pe.astype(BF16), w_pg.astype(BF16), row(ln2_g), row(ln2_b))


def kernel(x, p, w_in, b_in, conv_dw_w, conv_dw_b, conv_ln_g, conv_ln_b, cmp_pos_k, cmp_w1_k, cmp_w2_k, cmp_pos_v, cmp_w1_v, cmp_w2_v, w_out, b_out, ln1_g, ln1_b, w_up, b_up, w_down, b_down, w_pe, w_pg, ln2_g, ln2_b):
    params = (w_in, b_in, conv_dw_w, conv_dw_b, conv_ln_g, conv_ln_b,
              cmp_pos_k, cmp_w1_k, cmp_w2_k, cmp_pos_v, cmp_w1_v, cmp_w2_v,
              w_out, b_out, ln1_g, ln1_b, w_up, b_up, w_down, b_down, w_pe, w_pg, ln2_g, ln2_b)
    for i in range(DEPTH):
        x = _layer(x, p[i], *[t[i] for t in params])
    return x
```

```python
import functools

import jax
import jax.numpy as jnp
import numpy as np
from jax import lax
from jax.experimental import pallas as pl
from jax.experimental.pallas import tpu as pltpu

F32 = jnp.float32
BF16 = jnp.bfloat16

D_MODEL = 1024
PLE_DIM = 256
CONV_CH = 512
CONV_WIDTH = 31
HEAD_DIM = 64
N_KV = 2
HPG = 4
N_BRANCH = 3
CMP_LEN = 32
CMP_STRIDE = 16
CMP_HIDDEN = 256
SLC_BLOCK = 64
SLC_TOPN = 16
WINDOW = 512
D_FF = 4 * D_MODEL
LN_EPS = 1e-5
NEG_INF = -1e30
FORCE_BONUS = 1e4
DEPTH = 1
ALPHA = (2 * DEPTH) ** 0.25

UNSELECTED_BIAS = -float(2 ** 30)
BELOW_NEG_INF = -3e38

SUBLANES = 8
LANES = 128
PROJ_ROWS = 512
CONV_HALO = 32
CONV_ROW_CHUNK = 32
Q_TILE = 256
KEY_CHUNK = 256
WIN_UNIT = 128
CMP_UNIT = 128
LOG2_E = 1.4426950408889634
FFN_ROWS = 512
FFN_CHUNK = 1024
VMEM_LIMIT = 56 * 1024 * 1024

RM_COLS = 1792
VT_GROUP_ROWS = 80
VT_ROWS = N_KV * VT_GROUP_ROWS
T_ROWS = 512 + 2 * VT_ROWS + 32


def _layer_norm(z, g, b):
    mu = jnp.mean(z, axis=-1, keepdims=True)
    zc = z - mu
    var = jnp.mean(zc * zc, axis=-1, keepdims=True)
    return zc * lax.rsqrt(var + LN_EPS) * g + b


def _dot(a, b):
    return jnp.dot(a, b, preferred_element_type=F32)


def _dot_nt(a, b):
    return lax.dot_general(a, b, (((1,), (1,)), ((), ())), preferred_element_type=F32)


def _dot_tn(a, b):
    return lax.dot_general(a, b, (((0,), (0,)), ((), ())), preferred_element_type=F32)


def _causal_conv(xs_ref, w_ref, b_ref, g_ref, beta_ref, o_ref):
    n_shift = xs_ref.shape[1]
    x0 = xs_ref[0]
    for b in range(1, SUBLANES):
        xs_ref[b] = pltpu.roll(x0, n_shift - b, axis=0)
    lead = CONV_HALO - (CONV_WIDTH - 1)
    for r in range(0, n_shift - CONV_HALO, CONV_ROW_CHUNK):
        acc = jnp.zeros((CONV_ROW_CHUNK, CONV_CH), F32)
        for k in range(CONV_WIDTH):
            a, b = divmod(lead + k, SUBLANES)
            row = r + a * SUBLANES
            w_k = jnp.concatenate([w_ref[k]] * (CONV_ROW_CHUNK // SUBLANES), axis=0)
            acc = acc + xs_ref[b, row:row + CONV_ROW_CHUNK, :] * w_k
        y = _layer_norm(acc + b_ref[...], g_ref[...], beta_ref[...])
        o_ref[0, r:r + CONV_ROW_CHUNK, :] = (y * jax.nn.sigmoid(y)).astype(BF16)


def _proj_kernel(x_ref, wrm_ref, brm_ref, wt_ref, bt_ref, cw_ref, cb_ref, cg_ref, cbeta_ref,
                 yc_ref, kc_ref, vc_ref, ks_ref, kw_ref,
                 qT_ref, vsT_ref, vwT_ref, gT_ref, xs_ref, carry_ref):
    xb = x_ref[0].astype(BF16)

    def rm(c0, c1):
        return _dot(xb, wrm_ref[:, c0:c1]) + brm_ref[:, c0:c1]

    first_tile = pl.program_id(1) == 0
    xs_ref[0, 0:CONV_HALO, :] = jnp.where(first_tile, 0.0, carry_ref[...])
    half = CONV_CH // 2
    for c in range(0, CONV_CH, half):
        a = rm(c, c + half)
        g = rm(CONV_CH + c, CONV_CH + c + half)
        xs_ref[0, CONV_HALO:, c:c + half] = a * jax.nn.sigmoid(g)
    carry_ref[...] = xs_ref[0, PROJ_ROWS:PROJ_ROWS + CONV_HALO, :]

    _causal_conv(xs_ref, cw_ref, cb_ref, cg_ref, cbeta_ref, yc_ref)

    kv = rm(1024, 1280)
    kc_ref[0, 0] = kv[:, 0:64]
    kc_ref[0, 1] = kv[:, 64:128]
    vc_ref[0, 0] = kv[:, 128:192]
    vc_ref[0, 1] = kv[:, 192:256]

    kz = rm(1280, 1792).astype(BF16)
    ks_ref[0, 0] = kz[:, 0:128]
    ks_ref[0, 1] = kz[:, 128:256]
    kw_ref[0, 0] = kz[:, 256:384]
    kw_ref[0, 1] = kz[:, 384:512]

    def tr(r0, r1):
        return _dot_nt(wt_ref[r0:r1, :], xb) + bt_ref[r0:r1, :]

    qT_ref[0] = tr(0, 512).astype(BF16)
    r = 512
    vs = tr(r, r + VT_ROWS).astype(BF16)
    for u in range(PROJ_ROWS // KEY_CHUNK):
        vsT_ref[0, u] = vs[:, u * KEY_CHUNK:(u + 1) * KEY_CHUNK]
    r += VT_ROWS
    vw = tr(r, r + VT_ROWS).astype(BF16)
    for u in range(PROJ_ROWS // WIN_UNIT):
        vwT_ref[0, u] = vw[:, u * WIN_UNIT:(u + 1) * WIN_UNIT]
    r += VT_ROWS
    gT_ref[0] = jax.nn.sigmoid(tr(r, r + 32))


def _proj_call(x, wrm, brm, wt, bt, conv_w, conv_b, conv_g, conv_beta):
    B, S, D = x.shape
    tm = PROJ_ROWS
    n = S // tm
    const = lambda b, i: (0, 0)
    out_shape = (
        jax.ShapeDtypeStruct((B, S, CONV_CH), BF16),
        jax.ShapeDtypeStruct((B, N_KV, S, HEAD_DIM), F32),
        jax.ShapeDtypeStruct((B, N_KV, S, HEAD_DIM), F32),
        jax.ShapeDtypeStruct((B, N_KV, S, LANES), BF16),
        jax.ShapeDtypeStruct((B, N_KV, S, LANES), BF16),
        jax.ShapeDtypeStruct((B, 512, S), BF16),
        jax.ShapeDtypeStruct((B, S // KEY_CHUNK, VT_ROWS, KEY_CHUNK), BF16),
        jax.ShapeDtypeStruct((B, S // WIN_UNIT, VT_ROWS, WIN_UNIT), BF16),
        jax.ShapeDtypeStruct((B, 32, S), F32),
    )
    kvspec = pl.BlockSpec((1, N_KV, tm, HEAD_DIM), lambda b, i: (b, 0, i, 0))
    kzspec = pl.BlockSpec((1, N_KV, tm, LANES), lambda b, i: (b, 0, i, 0))
    out_specs = (
        pl.BlockSpec((1, tm, CONV_CH), lambda b, i: (b, i, 0)),
        kvspec, kvspec, kzspec, kzspec,
        pl.BlockSpec((1, 512, tm), lambda b, i: (b, 0, i)),
        pl.BlockSpec((1, tm // KEY_CHUNK, VT_ROWS, KEY_CHUNK), lambda b, i: (b, i, 0, 0)),
        pl.BlockSpec((1, tm // WIN_UNIT, VT_ROWS, WIN_UNIT), lambda b, i: (b, i, 0, 0)),
        pl.BlockSpec((1, 32, tm), lambda b, i: (b, 0, i)),
    )
    in_specs = [
        pl.BlockSpec((1, tm, D), lambda b, i: (b, i, 0)),
        pl.BlockSpec((D, RM_COLS), const),
        pl.BlockSpec((1, RM_COLS), const),
        pl.BlockSpec((T_ROWS, D), const),
        pl.BlockSpec((T_ROWS, 1), const),
        pl.BlockSpec((CONV_WIDTH, SUBLANES, CONV_CH), lambda b, i: (0, 0, 0)),
        pl.BlockSpec((1, CONV_CH), const), pl.BlockSpec((1, CONV_CH), const),
        pl.BlockSpec((1, CONV_CH), const),
    ]
    return pl.pallas_call(
        _proj_kernel, grid=(B, n), in_specs=in_specs, out_specs=out_specs,
        out_shape=out_shape, name="proj",
        compiler_params=pltpu.CompilerParams(
            dimension_semantics=("parallel", "arbitrary"), vmem_limit_bytes=VMEM_LIMIT),
        scratch_shapes=[pltpu.VMEM((SUBLANES, tm + CONV_HALO, CONV_CH), F32),
                        pltpu.VMEM((CONV_HALO, CONV_CH), F32)],
    )(x, wrm, brm, wt, bt, conv_w, conv_b, conv_g, conv_beta)


def _compress_hidden(r, pos_ref, w1_ref):
    half = CMP_STRIDE * HEAD_DIM
    n_rows = r.shape[0]
    a = _dot((r + pos_ref[0:1, :]).astype(BF16), w1_ref[0:half, :])
    b = _dot((r + pos_ref[1:2, :]).astype(BF16), w1_ref[half:2 * half, :])
    h = a + pltpu.roll(b, n_rows - 1, axis=0)
    return (h * jax.nn.sigmoid(h)).astype(BF16)


def _compress_kernel(rk_ref, rv_ref, pk_ref, w1k_ref, w2k_ref, pv_ref, w1v_ref, w2vT_ref,
                     kc_ref, vcT_ref):
    n_cmp = kc_ref.shape[2]

    def stride_rows(ref):
        return jnp.concatenate(
            [ref[0, 0, pl.ds(l, n_cmp, stride=CMP_STRIDE), :] for l in range(CMP_STRIDE)], axis=1)

    hk = _compress_hidden(stride_rows(rk_ref), pk_ref, w1k_ref)
    kc_ref[0, 0] = _dot(hk, w2k_ref[...]).astype(BF16)
    hv = _compress_hidden(stride_rows(rv_ref), pv_ref, w1v_ref)
    vcT_ref[0, 0] = _dot_nt(w2vT_ref[...], hv).astype(BF16)


def _compress_call(rk, rv, pk, w1k, w2k, pv, w1v, w2vT):
    B, G, S, _ = rk.shape
    NC = S // CMP_STRIDE
    W = CMP_STRIDE * HEAD_DIM
    const = lambda b, g: (0, 0)
    rspec = pl.BlockSpec((1, 1, S, HEAD_DIM), lambda b, g: (b, g, 0, 0))
    return pl.pallas_call(
        _compress_kernel, grid=(B, G),
        in_specs=[
            rspec, rspec,
            pl.BlockSpec((2, W), const), pl.BlockSpec((2 * W, CMP_HIDDEN), const),
            pl.BlockSpec((CMP_HIDDEN, LANES), const),
            pl.BlockSpec((2, W), const), pl.BlockSpec((2 * W, CMP_HIDDEN), const),
            pl.BlockSpec((HEAD_DIM, CMP_HIDDEN), const),
        ],
        out_specs=(
            pl.BlockSpec((1, 1, NC, LANES), lambda b, g: (b, g, 0, 0)),
            pl.BlockSpec((1, 1, HEAD_DIM, NC), lambda b, g: (b, g, 0, 0)),
        ),
        out_shape=(
            jax.ShapeDtypeStruct((B, G, NC, LANES), BF16),
            jax.ShapeDtypeStruct((B, G, HEAD_DIM, NC), BF16),
        ),
        name="compress",
        compiler_params=pltpu.CompilerParams(
            dimension_semantics=("parallel", "parallel"), vmem_limit_bytes=VMEM_LIMIT),
    )(rk, rv, pk, w1k, w2k, pv, w1v, w2vT)


def _nsa_kernel(q_ref, g_ref, kc_ref, vcT_ref, ks_ref, vsT_ref, kw_ref, vwT_ref,
                e_ref, kdead_ref, mov_ref, cbias_ref, wbias_ref, tbias_ref,
                o_ref, q2_ref, s_a, s_b, p_a, p_b, acc_ref, oc_ref, imp_ref):
    T = Q_TILE
    C = KEY_CHUNK
    qb = pl.program_id(1)
    t0 = qb * T
    n_cmp = kc_ref.shape[2]
    n_slc = mov_ref.shape[0]
    n_chunks = vsT_ref.shape[1]
    groups = range(N_KV)
    heads = range(HPG)
    cols = [(g, h) for g in groups for h in heads]

    def col(g, h):
        return g * HPG + h

    q4 = q_ref[0]
    spare = jnp.where(lax.broadcasted_iota(jnp.int32, (HEAD_DIM, T), 0) == 0,
                      UNSELECTED_BIAS, 0.0).astype(BF16)
    q1 = [jnp.concatenate([q4[i * HEAD_DIM:(i + 1) * HEAD_DIM, :], spare], axis=0)
          for i in range(len(cols))]
    q1_all = [jnp.concatenate(q1[g * HPG:(g + 1) * HPG], axis=1) for g in groups]
    pos_t = t0 + lax.broadcasted_iota(jnp.int32, (1, T), 1)

    c_off = pl.multiple_of(n_cmp - qb * (T // CMP_STRIDE), T // CMP_STRIDE)

    def all_heads(a):
        return jnp.concatenate([a] * HPG, axis=1)

    def head_cols(a):
        return [a[:, h * T:(h + 1) * T] for h in heads]

    def compressed(rows):
        for g in groups:
            sc = (_dot(kc_ref[0, g, 0:rows, :], q1_all[g])
                  + all_heads(cbias_ref[pl.ds(c_off, rows), :]))
            m_c = jnp.max(sc, axis=0, keepdims=True)
            p_c = jnp.exp2(sc - m_c)
            l_c = jnp.sum(p_c, axis=0, keepdims=True)
            p_c = p_c * jnp.where(m_c > 0.5 * NEG_INF, 1.0 / l_c, 0.0)
            oc_ref[g] = _dot(vcT_ref[0, g, :, 0:rows], p_c.astype(BF16))
            psum = functools.reduce(lambda a, b: a + b, head_cols(p_c))
            mov = mov_ref[:, 0:rows]
            p_hi = psum.astype(BF16)
            rem = psum - p_hi.astype(F32)
            p_mid = rem.astype(BF16)
            p_lo = (rem - p_mid.astype(F32)).astype(BF16)
            imp_ref[g] = _dot(mov, p_hi) + _dot(mov, p_mid) + _dot(mov, p_lo)

    cmp_units = (t0 + T - CMP_LEN) // CMP_STRIDE // CMP_UNIT + 1
    for units in range(1, n_cmp // CMP_UNIT + 1):
        pl.when(cmp_units == units)(functools.partial(compressed, units * CMP_UNIT))

    w_keys = WINDOW + T
    start = pl.multiple_of(jnp.maximum(t0 - WINDOW, 0), WIN_UNIT)
    w_off = pl.multiple_of(WINDOW - (t0 - start), WIN_UNIT)
    u0 = start // WIN_UNIT
    sw_raw = [_dot(kw_ref[0, g, pl.ds(start, w_keys), :], q1_all[g]) for g in groups]

    def exact_zero(tile):
        bits = pltpu.bitcast(tile, jnp.uint32)
        return pltpu.bitcast((bits >> 16) >> 16, F32)

    blk = lax.broadcasted_iota(jnp.int32, (n_slc, T), 0)
    blk_f = blk.astype(F32)
    cur = jnp.right_shift(pos_t, 6)
    forced = (blk == 0) | (blk == cur) | (blk == cur - 1)
    valid_b = (blk * SLC_BLOCK) <= pos_t
    bonus = jnp.where(forced, FORCE_BONUS, 0.0)
    work = [jnp.where(valid_b, imp_ref[g] + bonus, NEG_INF) for g in groups]
    for rnd in range(SLC_TOPN):
        for g in groups:
            mx = jnp.max(work[g], axis=0, keepdims=True)
            if rnd == (g + 1) * SLC_TOPN // (N_KV + 1):
                mx = mx + jnp.tile(exact_zero(sw_raw[g][0:SUBLANES, 0:LANES])[0:1], (1, T // LANES))
            first = jnp.min(jnp.where(work[g] == mx, blk_f, float(n_slc)), axis=0, keepdims=True)
            work[g] = jnp.where(blk_f == first, BELOW_NEG_INF, work[g])
    for g in groups:
        picked = work[g] < 0.5 * BELOW_NEG_INF
        unsel = jnp.where(picked & valid_b, 0.0, UNSELECTED_BIAS).astype(BF16)
        for h in heads:
            q2_ref[col(g, h)] = jnp.concatenate([q1[col(g, h)], unsel], axis=0)

    n_full = t0 // C
    uq = qb - n_full * (C // T)

    def key_operands(c, live):
        c = jnp.minimum(c, n_chunks - 1)
        off = pl.multiple_of(c * C, C)
        e_c = e_ref[pl.ds(off, C), :]
        out = []
        for g in groups:
            ks_c = ks_ref[0, g, pl.ds(off, C), :]
            if live is not None:
                ks_c = jnp.where(live, ks_c, kdead_ref[...])
            out.append(jnp.concatenate([ks_c, e_c], axis=1))
        return out

    def values(c):
        return [vsT_ref[0, c, g * VT_GROUP_ROWS:(g + 1) * VT_GROUP_ROWS, :] for g in groups]

    def scores(i, k2, sbuf, bias=None):
        s = _dot(k2, q2_ref[i])
        if bias is not None:
            s = s + bias
        sbuf[i] = s
        return jnp.max(s, axis=0, keepdims=True)

    def weights(i, sbuf, pbuf, cm, m_old):
        m_new = jnp.maximum(m_old, cm)
        pbuf[i] = jnp.exp2(sbuf[i] - m_new).astype(BF16)
        return m_new, jnp.exp2(m_old - m_new)

    def add_values(i, vt, pbuf, rescale):
        acc_ref[i] = rescale * acc_ref[i] + _dot(vt, pbuf[i])

    k2_t = key_operands(n_full, None)
    k2_0 = key_operands(0, 0 < n_full)
    t_off = pl.multiple_of((C // T - 1 - uq) * T, T)
    m0, cm0, cm_t = [], [], []
    for g, h in cols:
        i = col(g, h)
        cm_t.append(scores(i, k2_t[g], s_a, tbias_ref[pl.ds(t_off, C), :]))
        cm0.append(scores(i, k2_0[g], s_b))
        acc_ref[i] = jnp.zeros((VT_GROUP_ROWS, T), F32)

    scores_done = exact_zero(functools.reduce(lambda a, b: a + b, cm_t + cm0))
    o_w = []
    for g in groups:
        v_win = jnp.concatenate(
            [vwT_ref[0, u0 + u, g * VT_GROUP_ROWS:(g + 1) * VT_GROUP_ROWS, :]
             for u in range(w_keys // WIN_UNIT)], axis=1)
        sw = sw_raw[g] + all_heads(wbias_ref[pl.ds(w_off, w_keys), :])
        m_w = jnp.max(sw, axis=0, keepdims=True)
        if g == N_KV - 1:
            m_w = m_w + all_heads(scores_done)
        p_w = jnp.exp2(sw - m_w).astype(BF16)
        ow = _dot(v_win, p_w)
        o_w += head_cols(ow[0:HEAD_DIM] * (1.0 / ow[HEAD_DIM:HEAD_DIM + 1]))

    for i in range(len(cols)):
        m0.append(weights(i, s_a, p_a, cm_t[i], cm_t[i])[0])

    def pair_step(c0, carry):
        m, scale_pend, c_pend, cm_b = carry
        k2_a = key_operands(c0 + 1, c0 + 1 < n_full)
        k2_b = key_operands(c0 + 2, c0 + 2 < n_full)
        vt_pend = values(c_pend)
        vt_0 = values(c0)
        m1, scale0, cm_a = [], [], []
        for g, h in cols:
            i = col(g, h)
            add_values(i, vt_pend[g], p_a, scale_pend[i])
            cm_a.append(scores(i, k2_a[g], s_a))
            m_i, s_i = weights(i, s_b, p_b, cm_b[i], m[i])
            m1.append(m_i)
            scale0.append(s_i)
        m2, scale1, cm_b2 = [], [], []
        for g, h in cols:
            i = col(g, h)
            cm_b2.append(scores(i, k2_b[g], s_b))
            add_values(i, vt_0[g], p_b, scale0[i])
            m_i, s_i = weights(i, s_a, p_a, cm_a[i], m1[i])
            m2.append(m_i)
            scale1.append(s_i)
        return tuple(m2), tuple(scale1), jnp.minimum(c0 + 1, n_chunks - 1), tuple(cm_b2)

    ones = tuple(jnp.ones((1, T), F32) for _ in cols)
    n_quads = n_full // 4
    carry = lax.fori_loop(
        0, n_quads, lambda t, c: pair_step(4 * t + 2, pair_step(4 * t, c)),
        (tuple(m0), ones, n_full, tuple(cm0)))
    _, scale_pend, c_pend, _ = lax.fori_loop(
        0, (n_full - 4 * n_quads + 1) // 2, lambda t, c: pair_step(4 * n_quads + 2 * t, c), carry)

    vt_pend = values(c_pend)
    for g, h in cols:
        i = col(g, h)
        add_values(i, vt_pend[g], p_a, scale_pend[i])
        acc = acc_ref[i]
        o_s = acc[0:HEAD_DIM] * (1.0 / acc[HEAD_DIM:HEAD_DIM + 1])
        o_c = oc_ref[g, :, h * T:(h + 1) * T]
        gate = [g_ref[0, pl.ds(i * N_BRANCH + br, 1), :] for br in range(N_BRANCH)]
        out = gate[0] * o_c + gate[1] * o_s + gate[2] * o_w[i]
        o_ref[0, i * HEAD_DIM:(i + 1) * HEAD_DIM, :] = out.astype(BF16)


def _nsa_call(qT, gT, kc, vcT, ks, vsT, kw, vwT, consts):
    B, _, S = qT.shape
    G = N_KV
    T = Q_TILE
    C = KEY_CHUNK
    n_cmp = kc.shape[2]
    rows = G * HPG * HEAD_DIM
    n_cols = G * HPG
    e_mat, kdead, movT, cbias, wbias, tbias = consts
    once = pl.Buffered(1)
    per_batch = lambda shape: pl.BlockSpec((1,) + shape, lambda b, i: (b, 0, 0, 0), pipeline_mode=once)
    full = lambda a: pl.BlockSpec(a.shape, lambda b, i: (0, 0), pipeline_mode=once)
    in_specs = [
        pl.BlockSpec((1, rows, T), lambda b, i: (b, 0, i)),
        pl.BlockSpec((1, 32, T), lambda b, i: (b, 0, i)),
        per_batch((G, n_cmp, LANES)),
        per_batch((G, HEAD_DIM, n_cmp)),
        per_batch((G, S, LANES)),
        per_batch((S // C, VT_ROWS, C)),
        per_batch((G, S, LANES)),
        per_batch((S // WIN_UNIT, VT_ROWS, WIN_UNIT)),
        full(e_mat), full(kdead), full(movT), full(cbias), full(wbias), full(tbias),
    ]
    return pl.pallas_call(
        _nsa_kernel, grid=(B, S // T), in_specs=in_specs,
        out_specs=pl.BlockSpec((1, rows, T), lambda b, i: (b, 0, i)),
        out_shape=jax.ShapeDtypeStruct((B, rows, S), BF16),
        scratch_shapes=[pltpu.VMEM((n_cols, 2 * LANES, T), BF16),
                        pltpu.VMEM((n_cols, C, T), F32), pltpu.VMEM((n_cols, C, T), F32),
                        pltpu.VMEM((n_cols, C, T), BF16), pltpu.VMEM((n_cols, C, T), BF16),
                        pltpu.VMEM((n_cols, VT_GROUP_ROWS, T), F32),
                        pltpu.VMEM((G, HEAD_DIM, HPG * T), F32),
                        pltpu.VMEM((G, S // SLC_BLOCK, T), F32)],
        name="nsa",
        compiler_params=pltpu.CompilerParams(
            dimension_semantics=("parallel", "arbitrary"),
            vmem_limit_bytes=VMEM_LIMIT),
    )(qT, gT, kc, vcT, ks, vsT, kw, vwT, e_mat, kdead, movT, cbias, wbias, tbias)


def _mix_ffn_kernel(yc_ref, ynT_ref, x_ref, p_ref, wc_ref, wn_ref, bo_ref, g1_ref, beta1_ref,
                    wup_ref, bup_ref, wdn_ref, bdn_ref, wpe_ref, wpg_ref, g2_ref, beta2_ref,
                    o_ref, acc_ref):
    mix = _dot(yc_ref[0], wc_ref[...]) + _dot_tn(ynT_ref[0], wn_ref[...])
    x1 = _layer_norm(ALPHA * x_ref[0] + mix + bo_ref[...], g1_ref[...], beta1_ref[...])
    xb = x1.astype(BF16)
    ple = _dot(p_ref[0].astype(BF16), wpe_ref[...]) * jax.nn.sigmoid(_dot(xb, wpg_ref[...]))
    acc_ref[...] = ALPHA * x1 + ple + bdn_ref[...]
    for c in range(0, D_FF, FFN_CHUNK):
        u = _dot(xb, wup_ref[:, c:c + FFN_CHUNK]) + bup_ref[:, c:c + FFN_CHUNK]
        u = jnp.square(jnp.maximum(u, 0.0)).astype(BF16)
        acc_ref[...] += _dot(u, wdn_ref[c:c + FFN_CHUNK, :])
    o_ref[0] = _layer_norm(acc_ref[...], g2_ref[...], beta2_ref[...])


def _mix_ffn_call(yc, ynT, x, p, wc, wn, bo, g1, beta1, wup, bup, wdn, bdn, wpe, wpg, g2, beta2):
    B, S, D = x.shape
    tm = FFN_ROWS
    const = lambda b, i: (0, 0)
    rows = lambda width: pl.BlockSpec((1, tm, width), lambda b, i: (b, i, 0))
    weight = lambda shape: pl.BlockSpec(shape, const, pipeline_mode=pl.Buffered(1))
    vec = pl.BlockSpec((1, D), const)
    return pl.pallas_call(
        _mix_ffn_kernel, grid=(B, S // tm),
        in_specs=[
            rows(CONV_CH),
            pl.BlockSpec((1, D - CONV_CH, tm), lambda b, i: (b, 0, i)),
            rows(D), rows(PLE_DIM),
            weight((CONV_CH, D)), weight((D - CONV_CH, D)), vec, vec, vec,
            weight((D, D_FF)), pl.BlockSpec((1, D_FF), const), weight((D_FF, D)), vec,
            weight((PLE_DIM, D)), weight((D, D)), vec, vec,
        ],
        out_specs=rows(D),
        out_shape=jax.ShapeDtypeStruct((B, S, D), F32),
        scratch_shapes=[pltpu.VMEM((tm, D), F32)],
        name="mixffn",
        compiler_params=pltpu.CompilerParams(
            dimension_semantics=("parallel", "parallel"), vmem_limit_bytes=VMEM_LIMIT),
    )(yc, ynT, x, p, wc, wn, bo, g1, beta1, wup, bup, wdn, bdn, wpe, wpg, g2, beta2)


def _pad_groups(w):
    z = jnp.zeros(w.shape[:-1] + (HEAD_DIM,), w.dtype)
    return jnp.concatenate([w[..., :HEAD_DIM], z, w[..., HEAD_DIM:], z], axis=-1)


def _overlap_matrix_t(n_cmp_padded, n_slc):
    n_cmp = n_cmp_padded - 1
    c0 = np.arange(n_cmp) * CMP_STRIDE
    c1 = c0 + CMP_LEN - 1
    s0 = np.arange(n_slc) * SLC_BLOCK
    s1 = s0 + SLC_BLOCK - 1
    m = ((c0[:, None] <= s1[None, :]) & (c1[:, None] >= s0[None, :])).astype(np.float32)
    out = np.zeros((n_slc, n_cmp_padded), np.float32)
    out[:, :n_cmp] = m.T
    return out


def _vt_rows(w, bias):
    d = w.shape[0]
    pad = VT_GROUP_ROWS - HEAD_DIM
    one = jnp.zeros((pad,), F32).at[0].set(1.0)
    ws, bs = [], []
    for g in range(N_KV):
        ws += [w[:, g * HEAD_DIM:(g + 1) * HEAD_DIM], jnp.zeros((d, pad), F32)]
        bs += [bias[g * HEAD_DIM:(g + 1) * HEAD_DIM], one]
    return jnp.concatenate(ws, axis=1), jnp.concatenate(bs)


def _nsa_constants(S, n_cmp_p):
    T, C = Q_TILE, KEY_CHUNK
    t = np.arange(T)[None, :]
    neg = np.float32(NEG_INF)

    def bias(valid):
        return jnp.asarray(np.where(valid, np.float32(0.0), neg))

    e_mat = (np.arange(S)[:, None] // SLC_BLOCK == np.arange(S // SLC_BLOCK)[None, :])
    kdead = np.zeros((C, LANES), np.float32)
    kdead[:, HEAD_DIM] = 1.0
    j = np.arange(2 * n_cmp_p)[:, None]
    cbias = bias(CMP_STRIDE * (j - n_cmp_p) + CMP_LEN - 1 <= t)
    j = np.arange(2 * WINDOW + T)[:, None]
    wbias = bias((t < j) & (j <= WINDOW + t))
    j = np.arange(2 * C - T)[:, None]
    tbias = bias(j - (C - T) <= t)
    return (jnp.asarray(e_mat.astype(np.float32), BF16), jnp.asarray(kdead, BF16),
            jnp.asarray(_overlap_matrix_t(n_cmp_p, S // SLC_BLOCK), BF16), cbias, wbias, tbias)


def _layer(x, p, w_in, b_in, conv_dw_w, conv_dw_b, conv_ln_g, conv_ln_b,
           cmp_pos_k, cmp_w1_k, cmp_w2_k, cmp_pos_v, cmp_w1_v, cmp_w2_v,
           w_out, b_out, ln1_g, ln1_b, w_up, b_up, w_down, b_down, w_pe, w_pg, ln2_g, ln2_b):
    B, S, D = x.shape
    scale = HEAD_DIM ** -0.5 * LOG2_E
    row = lambda v: v.reshape(1, -1).astype(F32)

    w, bias = w_in, b_in
    wrm = jnp.concatenate(
        [w[:, 0:1024], w[:, 1536:1792], _pad_groups(w[:, 1792:1920]), _pad_groups(w[:, 2048:2176])],
        axis=1).astype(BF16)
    brm = row(jnp.concatenate(
        [bias[0:1024], bias[1536:1792], _pad_groups(bias[1792:1920]), _pad_groups(bias[2048:2176])]))
    w_vs, b_vs = _vt_rows(w[:, 1920:2048], bias[1920:2048])
    w_vw, b_vw = _vt_rows(w[:, 2176:2304], bias[2176:2304])
    wt = jnp.concatenate(
        [w[:, 1024:1536] * scale, w_vs, w_vw, w[:, 2304:2328], jnp.zeros((D, 8), F32)],
        axis=1).T.astype(BF16)
    bt = jnp.concatenate(
        [bias[1024:1536] * scale, b_vs, b_vw, bias[2304:2328], jnp.zeros((8,), F32)]).reshape(-1, 1)

    conv_w = jnp.broadcast_to(conv_dw_w.reshape(CONV_WIDTH, 1, CONV_CH),
                              (CONV_WIDTH, SUBLANES, CONV_CH))
    y_conv, kc, vc, ks, kw, qT, vsT, vwT, gT = _proj_call(
        x, wrm, brm, wt, bt, conv_w, row(conv_dw_b), row(conv_ln_g), row(conv_ln_b))

    n_cmp_p = S // CMP_STRIDE
    blk_w = CMP_STRIDE * HEAD_DIM
    w2k = jnp.concatenate([cmp_w2_k, jnp.zeros((CMP_HIDDEN, LANES - HEAD_DIM), F32)], axis=1)
    kcz, vcT = _compress_call(
        kc, vc, cmp_pos_k.reshape(2, blk_w), cmp_w1_k.astype(BF16), w2k.astype(BF16),
        cmp_pos_v.reshape(2, blk_w), cmp_w1_v.astype(BF16), cmp_w2_v.T.astype(BF16))

    y_nsaT = _nsa_call(qT, gT, kcz, vcT, ks, vsT, kw, vwT, _nsa_constants(S, n_cmp_p))

    wo = w_out.astype(BF16)
    return _mix_ffn_call(
        y_conv, y_nsaT, x, p, wo[:CONV_CH], wo[CONV_CH:], row(b_out), row(ln1_g), row(ln1_b),
        w_up.astype(BF16), row(b_up), w_down.astype(BF16), row(b_down),
        w_pe.astype(BF16), w_pg.astype(BF16), row(ln2_g), row(ln2_b))


def kernel(x, p, w_in, b_in, conv_dw_w, conv_dw_b, conv_ln_g, conv_ln_b, cmp_pos_k, cmp_w1_k, cmp_w2_k, cmp_pos_v, cmp_w1_v, cmp_w2_v, w_out, b_out, ln1_g, ln1_b, w_up, b_up, w_down, b_down, w_pe, w_pg, ln2_g, ln2_b):
    params = (w_in, b_in, conv_dw_w, conv_dw_b, conv_ln_g, conv_ln_b,
              cmp_pos_k, cmp_w1_k, cmp_w2_k, cmp_pos_v, cmp_w1_v, cmp_w2_v,
              w_out, b_out, ln1_g, ln1_b, w_up, b_up, w_down, b_down, w_pe, w_pg, ln2_g, ln2_b)
    for i in range(DEPTH):
        x = _layer(x, p[i], *[t[i] for t in params])
    return x
```

```python
import functools

import jax
import jax.numpy as jnp
import numpy as np
from jax import lax
from jax.experimental import pallas as pl
from jax.experimental.pallas import tpu as pltpu

F32 = jnp.float32
BF16 = jnp.bfloat16

D_MODEL = 1024
PLE_DIM = 256
CONV_CH = 512
CONV_WIDTH = 31
HEAD_DIM = 64
N_KV = 2
HPG = 4
N_BRANCH = 3
CMP_LEN = 32
CMP_STRIDE = 16
CMP_HIDDEN = 256
SLC_BLOCK = 64
SLC_TOPN = 16
WINDOW = 512
D_FF = 4 * D_MODEL
LN_EPS = 1e-5
NEG_INF = -1e30
FORCE_BONUS = 1e4
DEPTH = 1
ALPHA = (2 * DEPTH) ** 0.25

UNSELECTED_BIAS = -float(2 ** 30)
BELOW_NEG_INF = -3e38

SUBLANES = 8
LANES = 128
PROJ_ROWS = 512
CONV_HALO = 32
CONV_ROW_CHUNK = 32
Q_TILE = 256
KEY_CHUNK = 256
WIN_UNIT = 128
CMP_UNIT = 128
LOOP_SPANS = (8, 4, 2)
LOG2_E = 1.4426950408889634
FFN_ROWS = 512
FFN_CHUNK = 1024
VMEM_LIMIT = 56 * 1024 * 1024

RM_COLS = 1792
VT_GROUP_ROWS = 80
VT_ROWS = N_KV * VT_GROUP_ROWS
T_ROWS = 512 + 2 * VT_ROWS + 32


def _layer_norm(z, g, b):
    mu = jnp.mean(z, axis=-1, keepdims=True)
    zc = z - mu
    var = jnp.mean(zc * zc, axis=-1, keepdims=True)
    return zc * lax.rsqrt(var + LN_EPS) * g + b


def _dot(a, b):
    return jnp.dot(a, b, preferred_element_type=F32)


def _dot_nt(a, b):
    return lax.dot_general(a, b, (((1,), (1,)), ((), ())), preferred_element_type=F32)


def _dot_tn(a, b):
    return lax.dot_general(a, b, (((0,), (0,)), ((), ())), preferred_element_type=F32)


def _causal_conv(xs_ref, w_ref, b_ref, g_ref, beta_ref, o_ref):
    n_shift = xs_ref.shape[1]
    x0 = xs_ref[0]
    for b in range(1, SUBLANES):
        xs_ref[b] = pltpu.roll(x0, n_shift - b, axis=0)
    lead = CONV_HALO - (CONV_WIDTH - 1)
    for r in range(0, n_shift - CONV_HALO, CONV_ROW_CHUNK):
        acc = jnp.zeros((CONV_ROW_CHUNK, CONV_CH), F32)
        for k in range(CONV_WIDTH):
            a, b = divmod(lead + k, SUBLANES)
            row = r + a * SUBLANES
            w_k = jnp.concatenate([w_ref[k]] * (CONV_ROW_CHUNK // SUBLANES), axis=0)
            acc = acc + xs_ref[b, row:row + CONV_ROW_CHUNK, :] * w_k
        y = _layer_norm(acc + b_ref[...], g_ref[...], beta_ref[...])
        o_ref[0, r:r + CONV_ROW_CHUNK, :] = (y * jax.nn.sigmoid(y)).astype(BF16)


def _proj_kernel(x_ref, wrm_ref, brm_ref, wt_ref, bt_ref, cw_ref, cb_ref, cg_ref, cbeta_ref,
                 yc_ref, kc_ref, vc_ref, ks_ref, kw_ref,
                 qT_ref, vsT_ref, vwT_ref, gT_ref, xs_ref, carry_ref):
    xb = x_ref[0].astype(BF16)

    def rm(c0, c1):
        return _dot(xb, wrm_ref[:, c0:c1]) + brm_ref[:, c0:c1]

    first_tile = pl.program_id(1) == 0
    xs_ref[0, 0:CONV_HALO, :] = jnp.where(first_tile, 0.0, carry_ref[...])
    half = CONV_CH // 2
    for c in range(0, CONV_CH, half):
        a = rm(c, c + half)
        g = rm(CONV_CH + c, CONV_CH + c + half)
        xs_ref[0, CONV_HALO:, c:c + half] = a * jax.nn.sigmoid(g)
    carry_ref[...] = xs_ref[0, PROJ_ROWS:PROJ_ROWS + CONV_HALO, :]

    _causal_conv(xs_ref, cw_ref, cb_ref, cg_ref, cbeta_ref, yc_ref)

    kv = rm(1024, 1280)
    kc_ref[0, 0] = kv[:, 0:64]
    kc_ref[0, 1] = kv[:, 64:128]
    vc_ref[0, 0] = kv[:, 128:192]
    vc_ref[0, 1] = kv[:, 192:256]

    kz = rm(1280, 1792).astype(BF16)
    ks_ref[0, 0] = kz[:, 0:128]
    ks_ref[0, 1] = kz[:, 128:256]
    kw_ref[0, 0] = kz[:, 256:384]
    kw_ref[0, 1] = kz[:, 384:512]

    def tr(r0, r1):
        return _dot_nt(wt_ref[r0:r1, :], xb) + bt_ref[r0:r1, :]

    qT_ref[0] = tr(0, 512).astype(BF16)
    r = 512
    vs = tr(r, r + VT_ROWS).astype(BF16)
    for u in range(PROJ_ROWS // KEY_CHUNK):
        vsT_ref[0, u] = vs[:, u * KEY_CHUNK:(u + 1) * KEY_CHUNK]
    r += VT_ROWS
    vw = tr(r, r + VT_ROWS).astype(BF16)
    for u in range(PROJ_ROWS // WIN_UNIT):
        vwT_ref[0, u] = vw[:, u * WIN_UNIT:(u + 1) * WIN_UNIT]
    r += VT_ROWS
    gT_ref[0] = jax.nn.sigmoid(tr(r, r + 32))


def _proj_call(x, wrm, brm, wt, bt, conv_w, conv_b, conv_g, conv_beta):
    B, S, D = x.shape
    tm = PROJ_ROWS
    n = S // tm
    const = lambda b, i: (0, 0)
    out_shape = (
        jax.ShapeDtypeStruct((B, S, CONV_CH), BF16),
        jax.ShapeDtypeStruct((B, N_KV, S, HEAD_DIM), F32),
        jax.ShapeDtypeStruct((B, N_KV, S, HEAD_DIM), F32),
        jax.ShapeDtypeStruct((B, N_KV, S, LANES), BF16),
        jax.ShapeDtypeStruct((B, N_KV, S, LANES), BF16),
        jax.ShapeDtypeStruct((B, 512, S), BF16),
        jax.ShapeDtypeStruct((B, S // KEY_CHUNK, VT_ROWS, KEY_CHUNK), BF16),
        jax.ShapeDtypeStruct((B, S // WIN_UNIT, VT_ROWS, WIN_UNIT), BF16),
        jax.ShapeDtypeStruct((B, 32, S), F32),
    )
    kvspec = pl.BlockSpec((1, N_KV, tm, HEAD_DIM), lambda b, i: (b, 0, i, 0))
    kzspec = pl.BlockSpec((1, N_KV, tm, LANES), lambda b, i: (b, 0, i, 0))
    out_specs = (
        pl.BlockSpec((1, tm, CONV_CH), lambda b, i: (b, i, 0)),
        kvspec, kvspec, kzspec, kzspec,
        pl.BlockSpec((1, 512, tm), lambda b, i: (b, 0, i)),
        pl.BlockSpec((1, tm // KEY_CHUNK, VT_ROWS, KEY_CHUNK), lambda b, i: (b, i, 0, 0)),
        pl.BlockSpec((1, tm // WIN_UNIT, VT_ROWS, WIN_UNIT), lambda b, i: (b, i, 0, 0)),
        pl.BlockSpec((1, 32, tm), lambda b, i: (b, 0, i)),
    )
    in_specs = [
        pl.BlockSpec((1, tm, D), lambda b, i: (b, i, 0)),
        pl.BlockSpec((D, RM_COLS), const),
        pl.BlockSpec((1, RM_COLS), const),
        pl.BlockSpec((T_ROWS, D), const),
        pl.BlockSpec((T_ROWS, 1), const),
        pl.BlockSpec((CONV_WIDTH, SUBLANES, CONV_CH), lambda b, i: (0, 0, 0)),
        pl.BlockSpec((1, CONV_CH), const), pl.BlockSpec((1, CONV_CH), const),
        pl.BlockSpec((1, CONV_CH), const),
    ]
    return pl.pallas_call(
        _proj_kernel, grid=(B, n), in_specs=in_specs, out_specs=out_specs,
        out_shape=out_shape, name="proj",
        compiler_params=pltpu.CompilerParams(
            dimension_semantics=("parallel", "arbitrary"), vmem_limit_bytes=VMEM_LIMIT),
        scratch_shapes=[pltpu.VMEM((SUBLANES, tm + CONV_HALO, CONV_CH), F32),
                        pltpu.VMEM((CONV_HALO, CONV_CH), F32)],
    )(x, wrm, brm, wt, bt, conv_w, conv_b, conv_g, conv_beta)


def _compress_hidden(r, pos_ref, w1_ref):
    half = CMP_STRIDE * HEAD_DIM
    n_rows = r.shape[0]
    a = _dot((r + pos_ref[0:1, :]).astype(BF16), w1_ref[0:half, :])
    b = _dot((r + pos_ref[1:2, :]).astype(BF16), w1_ref[half:2 * half, :])
    h = a + pltpu.roll(b, n_rows - 1, axis=0)
    return (h * jax.nn.sigmoid(h)).astype(BF16)


def _compress_kernel(rk_ref, rv_ref, pk_ref, w1k_ref, w2k_ref, pv_ref, w1v_ref, w2vT_ref,
                     kc_ref, vcT_ref):
    n_cmp = kc_ref.shape[2]

    def stride_rows(ref):
        return jnp.concatenate(
            [ref[0, 0, pl.ds(l, n_cmp, stride=CMP_STRIDE), :] for l in range(CMP_STRIDE)], axis=1)

    hk = _compress_hidden(stride_rows(rk_ref), pk_ref, w1k_ref)
    kc_ref[0, 0] = _dot(hk, w2k_ref[...]).astype(BF16)
    hv = _compress_hidden(stride_rows(rv_ref), pv_ref, w1v_ref)
    vcT_ref[0, 0] = _dot_nt(w2vT_ref[...], hv).astype(BF16)


def _compress_call(rk, rv, pk, w1k, w2k, pv, w1v, w2vT):
    B, G, S, _ = rk.shape
    NC = S // CMP_STRIDE
    W = CMP_STRIDE * HEAD_DIM
    const = lambda b, g: (0, 0)
    rspec = pl.BlockSpec((1, 1, S, HEAD_DIM), lambda b, g: (b, g, 0, 0))
    return pl.pallas_call(
        _compress_kernel, grid=(B, G),
        in_specs=[
            rspec, rspec,
            pl.BlockSpec((2, W), const), pl.BlockSpec((2 * W, CMP_HIDDEN), const),
            pl.BlockSpec((CMP_HIDDEN, LANES), const),
            pl.BlockSpec((2, W), const), pl.BlockSpec((2 * W, CMP_HIDDEN), const),
            pl.BlockSpec((HEAD_DIM, CMP_HIDDEN), const),
        ],
        out_specs=(
            pl.BlockSpec((1, 1, NC, LANES), lambda b, g: (b, g, 0, 0)),
            pl.BlockSpec((1, 1, HEAD_DIM, NC), lambda b, g: (b, g, 0, 0)),
        ),
        out_shape=(
            jax.ShapeDtypeStruct((B, G, NC, LANES), BF16),
            jax.ShapeDtypeStruct((B, G, HEAD_DIM, NC), BF16),
        ),
        name="compress",
        compiler_params=pltpu.CompilerParams(
            dimension_semantics=("parallel", "parallel"), vmem_limit_bytes=VMEM_LIMIT),
    )(rk, rv, pk, w1k, w2k, pv, w1v, w2vT)


def _nsa_kernel(q_ref, g_ref, kc_ref, vcT_ref, ks_ref, vsT_ref, kw_ref, vwT_ref,
                e_ref, kdead_ref, mov_ref, cbias_ref, wbias_ref, tbias_ref,
                o_ref, q2_ref, s_a, s_b, p_a, p_b, acc_ref, oc_ref, imp_ref):
    T = Q_TILE
    C = KEY_CHUNK
    qb = pl.program_id(1)
    t0 = qb * T
    n_cmp = kc_ref.shape[2]
    n_slc = mov_ref.shape[0]
    n_chunks = vsT_ref.shape[1]
    groups = range(N_KV)
    heads = range(HPG)
    cols = [(g, h) for g in groups for h in heads]

    def col(g, h):
        return g * HPG + h

    q4 = q_ref[0]
    spare = jnp.where(lax.broadcasted_iota(jnp.int32, (HEAD_DIM, T), 0) == 0,
                      UNSELECTED_BIAS, 0.0).astype(BF16)
    q1 = [jnp.concatenate([q4[i * HEAD_DIM:(i + 1) * HEAD_DIM, :], spare], axis=0)
          for i in range(len(cols))]
    q1_all = [jnp.concatenate(q1[g * HPG:(g + 1) * HPG], axis=1) for g in groups]
    pos_t = t0 + lax.broadcasted_iota(jnp.int32, (1, T), 1)

    c_off = pl.multiple_of(n_cmp - qb * (T // CMP_STRIDE), T // CMP_STRIDE)

    def all_heads(a):
        return jnp.concatenate([a] * HPG, axis=1)

    def head_cols(a):
        return [a[:, h * T:(h + 1) * T] for h in heads]

    def compressed(rows):
        for g in groups:
            sc = (_dot(kc_ref[0, g, 0:rows, :], q1_all[g])
                  + all_heads(cbias_ref[pl.ds(c_off, rows), :]))
            m_c = jnp.max(sc, axis=0, keepdims=True)
            p_c = jnp.exp2(sc - m_c)
            l_c = jnp.sum(p_c, axis=0, keepdims=True)
            p_c = p_c * jnp.where(m_c > 0.5 * NEG_INF, 1.0 / l_c, 0.0)
            oc_ref[g] = _dot(vcT_ref[0, g, :, 0:rows], p_c.astype(BF16))
            psum = functools.reduce(lambda a, b: a + b, head_cols(p_c))
            mov = mov_ref[:, 0:rows]
            p_hi = psum.astype(BF16)
            rem = psum - p_hi.astype(F32)
            p_mid = rem.astype(BF16)
            p_lo = (rem - p_mid.astype(F32)).astype(BF16)
            imp_ref[g] = _dot(mov, p_hi) + _dot(mov, p_mid) + _dot(mov, p_lo)

    cmp_units = (t0 + T - CMP_LEN) // CMP_STRIDE // CMP_UNIT + 1
    for units in range(1, n_cmp // CMP_UNIT + 1):
        pl.when(cmp_units == units)(functools.partial(compressed, units * CMP_UNIT))

    w_keys = WINDOW + T
    start = pl.multiple_of(jnp.maximum(t0 - WINDOW, 0), WIN_UNIT)
    w_off = pl.multiple_of(WINDOW - (t0 - start), WIN_UNIT)
    u0 = start // WIN_UNIT
    sw_raw = [_dot(kw_ref[0, g, pl.ds(start, w_keys), :], q1_all[g]) for g in groups]

    def exact_zero(tile):
        bits = pltpu.bitcast(tile, jnp.uint32)
        return pltpu.bitcast((bits >> 16) >> 16, F32)

    blk = lax.broadcasted_iota(jnp.int32, (n_slc, T), 0)
    blk_f = blk.astype(F32)
    cur = jnp.right_shift(pos_t, 6)
    forced = (blk == 0) | (blk == cur) | (blk == cur - 1)
    valid_b = (blk * SLC_BLOCK) <= pos_t
    bonus = jnp.where(forced, FORCE_BONUS, 0.0)
    work = [jnp.where(valid_b, imp_ref[g] + bonus, NEG_INF) for g in groups]
    for rnd in range(SLC_TOPN):
        for g in groups:
            mx = jnp.max(work[g], axis=0, keepdims=True)
            if rnd == (g + 1) * SLC_TOPN // (N_KV + 1):
                mx = mx + jnp.tile(exact_zero(sw_raw[g][0:SUBLANES, 0:LANES])[0:1], (1, T // LANES))
            first = jnp.min(jnp.where(work[g] == mx, blk_f, float(n_slc)), axis=0, keepdims=True)
            work[g] = jnp.where(blk_f == first, BELOW_NEG_INF, work[g])
    for g in groups:
        picked = work[g] < 0.5 * BELOW_NEG_INF
        unsel = jnp.where(picked & valid_b, 0.0, UNSELECTED_BIAS).astype(BF16)
        for h in heads:
            q2_ref[col(g, h)] = jnp.concatenate([q1[col(g, h)], unsel], axis=0)

    n_full = t0 // C
    uq = qb - n_full * (C // T)

    def key_operands(c, live):
        c = jnp.minimum(c, n_chunks - 1)
        off = pl.multiple_of(c * C, C)
        e_c = e_ref[pl.ds(off, C), :]
        out = []
        for g in groups:
            ks_c = ks_ref[0, g, pl.ds(off, C), :]
            if live is not None:
                ks_c = jnp.where(live, ks_c, kdead_ref[...])
            out.append(jnp.concatenate([ks_c, e_c], axis=1))
        return out

    def values(c):
        return [vsT_ref[0, c, g * VT_GROUP_ROWS:(g + 1) * VT_GROUP_ROWS, :] for g in groups]

    def scores(i, k2, sbuf, bias=None):
        s = _dot(k2, q2_ref[i])
        if bias is not None:
            s = s + bias
        sbuf[i] = s
        return jnp.max(s, axis=0, keepdims=True)

    def weights(i, sbuf, pbuf, cm, m_old):
        m_new = jnp.maximum(m_old, cm)
        pbuf[i] = jnp.exp2(sbuf[i] - m_new).astype(BF16)
        return m_new, jnp.exp2(m_old - m_new)

    def add_values(i, vt, pbuf, rescale):
        acc_ref[i] = rescale * acc_ref[i] + _dot(vt, pbuf[i])

    k2_t = key_operands(n_full, None)
    k2_0 = key_operands(0, 0 < n_full)
    t_off = pl.multiple_of((C // T - 1 - uq) * T, T)
    m0, cm0, cm_t = [], [], []
    for g, h in cols:
        i = col(g, h)
        cm_t.append(scores(i, k2_t[g], s_a, tbias_ref[pl.ds(t_off, C), :]))
        cm0.append(scores(i, k2_0[g], s_b))
        acc_ref[i] = jnp.zeros((VT_GROUP_ROWS, T), F32)

    scores_done = exact_zero(functools.reduce(lambda a, b: a + b, cm_t + cm0))
    o_w = []
    for g in groups:
        v_win = jnp.concatenate(
            [vwT_ref[0, u0 + u, g * VT_GROUP_ROWS:(g + 1) * VT_GROUP_ROWS, :]
             for u in range(w_keys // WIN_UNIT)], axis=1)
        sw = sw_raw[g] + all_heads(wbias_ref[pl.ds(w_off, w_keys), :])
        m_w = jnp.max(sw, axis=0, keepdims=True)
        if g == N_KV - 1:
            m_w = m_w + all_heads(scores_done)
        p_w = jnp.exp2(sw - m_w).astype(BF16)
        ow = _dot(v_win, p_w)
        o_w += head_cols(ow[0:HEAD_DIM] * (1.0 / ow[HEAD_DIM:HEAD_DIM + 1]))

    for i in range(len(cols)):
        m0.append(weights(i, s_a, p_a, cm_t[i], cm_t[i])[0])

    def pair_step(c0, carry):
        m, scale_pend, c_pend, cm_b = carry
        k2_a = key_operands(c0 + 1, c0 + 1 < n_full)
        k2_b = key_operands(c0 + 2, c0 + 2 < n_full)
        vt_pend = values(c_pend)
        vt_0 = values(c0)
        m1, scale0, cm_a = [], [], []
        for g, h in cols:
            i = col(g, h)
            add_values(i, vt_pend[g], p_a, scale_pend[i])
            cm_a.append(scores(i, k2_a[g], s_a))
            m_i, s_i = weights(i, s_b, p_b, cm_b[i], m[i])
            m1.append(m_i)
            scale0.append(s_i)
        m2, scale1, cm_b2 = [], [], []
        for g, h in cols:
            i = col(g, h)
            cm_b2.append(scores(i, k2_b[g], s_b))
            add_values(i, vt_0[g], p_b, scale0[i])
            m_i, s_i = weights(i, s_a, p_a, cm_a[i], m1[i])
            m2.append(m_i)
            scale1.append(s_i)
        return tuple(m2), tuple(scale1), jnp.minimum(c0 + 1, n_chunks - 1), tuple(cm_b2)

    carry = (tuple(m0), tuple(jnp.ones((1, T), F32) for _ in cols), n_full, tuple(cm0))
    done = 0
    for span in LOOP_SPANS:
        left = n_full - done
        trips = (left + 1) // 2 if span == 2 else left // span

        def trip(t, c, span=span, done=done):
            for j in range(0, span, 2):
                c = pair_step(done + span * t + j, c)
            return c

        carry = lax.fori_loop(0, trips, trip, carry)
        done = done + trips * span
    _, scale_pend, c_pend, _ = carry

    vt_pend = values(c_pend)
    for g, h in cols:
        i = col(g, h)
        add_values(i, vt_pend[g], p_a, scale_pend[i])
        acc = acc_ref[i]
        o_s = acc[0:HEAD_DIM] * (1.0 / acc[HEAD_DIM:HEAD_DIM + 1])
        o_c = oc_ref[g, :, h * T:(h + 1) * T]
        gate = [g_ref[0, pl.ds(i * N_BRANCH + br, 1), :] for br in range(N_BRANCH)]
        out = gate[0] * o_c + gate[1] * o_s + gate[2] * o_w[i]
        o_ref[0, i * HEAD_DIM:(i + 1) * HEAD_DIM, :] = out.astype(BF16)


def _nsa_call(qT, gT, kc, vcT, ks, vsT, kw, vwT, consts):
    B, _, S = qT.shape
    G = N_KV
    T = Q_TILE
    C = KEY_CHUNK
    n_cmp = kc.shape[2]
    rows = G * HPG * HEAD_DIM
    n_cols = G * HPG
    e_mat, kdead, movT, cbias, wbias, tbias = consts
    once = pl.Buffered(1)
    per_batch = lambda shape: pl.BlockSpec((1,) + shape, lambda b, i: (b, 0, 0, 0), pipeline_mode=once)
    full = lambda a: pl.BlockSpec(a.shape, lambda b, i: (0, 0), pipeline_mode=once)
    in_specs = [
        pl.BlockSpec((1, rows, T), lambda b, i: (b, 0, i)),
        pl.BlockSpec((1, 32, T), lambda b, i: (b, 0, i)),
        per_batch((G, n_cmp, LANES)),
        per_batch((G, HEAD_DIM, n_cmp)),
        per_batch((G, S, LANES)),
        per_batch((S // C, VT_ROWS, C)),
        per_batch((G, S, LANES)),
        per_batch((S // WIN_UNIT, VT_ROWS, WIN_UNIT)),
        full(e_mat), full(kdead), full(movT), full(cbias), full(wbias), full(tbias),
    ]
    return pl.pallas_call(
        _nsa_kernel, grid=(B, S // T), in_specs=in_specs,
        out_specs=pl.BlockSpec((1, rows, T), lambda b, i: (b, 0, i)),
        out_shape=jax.ShapeDtypeStruct((B, rows, S), BF16),
        scratch_shapes=[pltpu.VMEM((n_cols, 2 * LANES, T), BF16),
                        pltpu.VMEM((n_cols, C, T), F32), pltpu.VMEM((n_cols, C, T), F32),
                        pltpu.VMEM((n_cols, C, T), BF16), pltpu.VMEM((n_cols, C, T), BF16),
                        pltpu.VMEM((n_cols, VT_GROUP_ROWS, T), F32),
                        pltpu.VMEM((G, HEAD_DIM, HPG * T), F32),
                        pltpu.VMEM((G, S // SLC_BLOCK, T), F32)],
        name="nsa",
        compiler_params=pltpu.CompilerParams(
            dimension_semantics=("parallel", "arbitrary"),
            vmem_limit_bytes=VMEM_LIMIT),
    )(qT, gT, kc, vcT, ks, vsT, kw, vwT, e_mat, kdead, movT, cbias, wbias, tbias)


def _mix_ffn_kernel(yc_ref, ynT_ref, x_ref, p_ref, wc_ref, wn_ref, bo_ref, g1_ref, beta1_ref,
                    wup_ref, bup_ref, wdn_ref, bdn_ref, wpe_ref, wpg_ref, g2_ref, beta2_ref,
                    o_ref, acc_ref):
    mix = _dot(yc_ref[0], wc_ref[...]) + _dot_tn(ynT_ref[0], wn_ref[...])
    x1 = _layer_norm(ALPHA * x_ref[0] + mix + bo_ref[...], g1_ref[...], beta1_ref[...])
    xb = x1.astype(BF16)
    ple = _dot(p_ref[0].astype(BF16), wpe_ref[...]) * jax.nn.sigmoid(_dot(xb, wpg_ref[...]))
    acc_ref[...] = ALPHA * x1 + ple + bdn_ref[...]
    for c in range(0, D_FF, FFN_CHUNK):
        u = _dot(xb, wup_ref[:, c:c + FFN_CHUNK]) + bup_ref[:, c:c + FFN_CHUNK]
        u = jnp.square(jnp.maximum(u, 0.0)).astype(BF16)
        acc_ref[...] += _dot(u, wdn_ref[c:c + FFN_CHUNK, :])
    o_ref[0] = _layer_norm(acc_ref[...], g2_ref[...], beta2_ref[...])


def _mix_ffn_call(yc, ynT, x, p, wc, wn, bo, g1, beta1, wup, bup, wdn, bdn, wpe, wpg, g2, beta2):
    B, S, D = x.shape
    tm = FFN_ROWS
    const = lambda b, i: (0, 0)
    rows = lambda width: pl.BlockSpec((1, tm, width), lambda b, i: (b, i, 0))
    weight = lambda shape: pl.BlockSpec(shape, const, pipeline_mode=pl.Buffered(1))
    vec = pl.BlockSpec((1, D), const)
    return pl.pallas_call(
        _mix_ffn_kernel, grid=(B, S // tm),
        in_specs=[
            rows(CONV_CH),
            pl.BlockSpec((1, D - CONV_CH, tm), lambda b, i: (b, 0, i)),
            rows(D), rows(PLE_DIM),
            weight((CONV_CH, D)), weight((D - CONV_CH, D)), vec, vec, vec,
            weight((D, D_FF)), pl.BlockSpec((1, D_FF), const), weight((D_FF, D)), vec,
            weight((PLE_DIM, D)), weight((D, D)), vec, vec,
        ],
        out_specs=rows(D),
        out_shape=jax.ShapeDtypeStruct((B, S, D), F32),
        scratch_shapes=[pltpu.VMEM((tm, D), F32)],
        name="mixffn",
        compiler_params=pltpu.CompilerParams(
            dimension_semantics=("parallel", "parallel"), vmem_limit_bytes=VMEM_LIMIT),
    )(yc, ynT, x, p, wc, wn, bo, g1, beta1, wup, bup, wdn, bdn, wpe, wpg, g2, beta2)


def _pad_groups(w):
    z = jnp.zeros(w.shape[:-1] + (HEAD_DIM,), w.dtype)
    return jnp.concatenate([w[..., :HEAD_DIM], z, w[..., HEAD_DIM:], z], axis=-1)


def _overlap_matrix_t(n_cmp_padded, n_slc):
    n_cmp = n_cmp_padded - 1
    c0 = np.arange(n_cmp) * CMP_STRIDE
    c1 = c0 + CMP_LEN - 1
    s0 = np.arange(n_slc) * SLC_BLOCK
    s1 = s0 + SLC_BLOCK - 1
    m = ((c0[:, None] <= s1[None, :]) & (c1[:, None] >= s0[None, :])).astype(np.float32)
    out = np.zeros((n_slc, n_cmp_padded), np.float32)
    out[:, :n_cmp] = m.T
    return out


def _vt_rows(w, bias):
    d = w.shape[0]
    pad = VT_GROUP_ROWS - HEAD_DIM
    one = jnp.zeros((pad,), F32).at[0].set(1.0)
    ws, bs = [], []
    for g in range(N_KV):
        ws += [w[:, g * HEAD_DIM:(g + 1) * HEAD_DIM], jnp.zeros((d, pad), F32)]
        bs += [bias[g * HEAD_DIM:(g + 1) * HEAD_DIM], one]
    return jnp.concatenate(ws, axis=1), jnp.concatenate(bs)


def _nsa_constants(S, n_cmp_p):
    T, C = Q_TILE, KEY_CHUNK
    t = np.arange(T)[None, :]
    neg = np.float32(NEG_INF)

    def bias(valid):
        return jnp.asarray(np.where(valid, np.float32(0.0), neg))

    e_mat = (np.arange(S)[:, None] // SLC_BLOCK == np.arange(S // SLC_BLOCK)[None, :])
    kdead = np.zeros((C, LANES), np.float32)
    kdead[:, HEAD_DIM] = 1.0
    j = np.arange(2 * n_cmp_p)[:, None]
    cbias = bias(CMP_STRIDE * (j - n_cmp_p) + CMP_LEN - 1 <= t)
    j = np.arange(2 * WINDOW + T)[:, None]
    wbias = bias((t < j) & (j <= WINDOW + t))
    j = np.arange(2 * C - T)[:, None]
    tbias = bias(j - (C - T) <= t)
    return (jnp.asarray(e_mat.astype(np.float32), BF16), jnp.asarray(kdead, BF16),
            jnp.asarray(_overlap_matrix_t(n_cmp_p, S // SLC_BLOCK), BF16), cbias, wbias, tbias)


def _layer(x, p, w_in, b_in, conv_dw_w, conv_dw_b, conv_ln_g, conv_ln_b,
           cmp_pos_k, cmp_w1_k, cmp_w2_k, cmp_pos_v, cmp_w1_v, cmp_w2_v,
           w_out, b_out, ln1_g, ln1_b, w_up, b_up, w_down, b_down, w_pe, w_pg, ln2_g, ln2_b):
    B, S, D = x.shape
    scale = HEAD_DIM ** -0.5 * LOG2_E
    row = lambda v: v.reshape(1, -1).astype(F32)

    w, bias = w_in, b_in
    wrm = jnp.concatenate(
        [w[:, 0:1024], w[:, 1536:1792], _pad_groups(w[:, 1792:1920]), _pad_groups(w[:, 2048:2176])],
        axis=1).astype(BF16)
    brm = row(jnp.concatenate(
        [bias[0:1024], bias[1536:1792], _pad_groups(bias[1792:1920]), _pad_groups(bias[2048:2176])]))
    w_vs, b_vs = _vt_rows(w[:, 1920:2048], bias[1920:2048])
    w_vw, b_vw = _vt_rows(w[:, 2176:2304], bias[2176:2304])
    wt = jnp.concatenate(
        [w[:, 1024:1536] * scale, w_vs, w_vw, w[:, 2304:2328], jnp.zeros((D, 8), F32)],
        axis=1).T.astype(BF16)
    bt = jnp.concatenate(
        [bias[1024:1536] * scale, b_vs, b_vw, bias[2304:2328], jnp.zeros((8,), F32)]).reshape(-1, 1)

    conv_w = jnp.broadcast_to(conv_dw_w.reshape(CONV_WIDTH, 1, CONV_CH),
                              (CONV_WIDTH, SUBLANES, CONV_CH))
    y_conv, kc, vc, ks, kw, qT, vsT, vwT, gT = _proj_call(
        x, wrm, brm, wt, bt, conv_w, row(conv_dw_b), row(conv_ln_g), row(conv_ln_b))

    n_cmp_p = S // CMP_STRIDE
    blk_w = CMP_STRIDE * HEAD_DIM
    w2k = jnp.concatenate([cmp_w2_k, jnp.zeros((CMP_HIDDEN, LANES - HEAD_DIM), F32)], axis=1)
    kcz, vcT = _compress_call(
        kc, vc, cmp_pos_k.reshape(2, blk_w), cmp_w1_k.astype(BF16), w2k.astype(BF16),
        cmp_pos_v.reshape(2, blk_w), cmp_w1_v.astype(BF16), cmp_w2_v.T.astype(BF16))

    y_nsaT = _nsa_call(qT, gT, kcz, vcT, ks, vsT, kw, vwT, _nsa_constants(S, n_cmp_p))

    wo = w_out.astype(BF16)
    return _mix_ffn_call(
        y_conv, y_nsaT, x, p, wo[:CONV_CH], wo[CONV_CH:], row(b_out), row(ln1_g), row(ln1_b),
        w_up.astype(BF16), row(b_up), w_down.astype(BF16), row(b_down),
        w_pe.astype(BF16), w_pg.astype(BF16), row(ln2_g), row(ln2_b))


def kernel(x, p, w_in, b_in, conv_dw_w, conv_dw_b, conv_ln_g, conv_ln_b, cmp_pos_k, cmp_w1_k, cmp_w2_k, cmp_pos_v, cmp_w1_v, cmp_w2_v, w_out, b_out, ln1_g, ln1_b, w_up, b_up, w_down, b_down, w_pe, w_pg, ln2_g, ln2_b):
    params = (w_in, b_in, conv_dw_w, conv_dw_b, conv_ln_g, conv_ln_b,
              cmp_pos_k, cmp_w1_k, cmp_w2_k, cmp_pos_v, cmp_w1_v, cmp_w2_v,
              w_out, b_out, ln1_g, ln1_b, w_up, b_up, w_down, b_down, w_pe, w_pg, ln2_g, ln2_b)
    for i in range(DEPTH):
        x = _layer(x, p[i], *[t[i] for t in params])
    return x
```

```python
import functools

import jax
import jax.numpy as jnp
import numpy as np
from jax import lax
from jax.experimental import pallas as pl
from jax.experimental.pallas import tpu as pltpu

F32 = jnp.float32
BF16 = jnp.bfloat16

D_MODEL = 1024
PLE_DIM = 256
CONV_CH = 512
CONV_WIDTH = 31
HEAD_DIM = 64
N_KV = 2
HPG = 4
N_BRANCH = 3
CMP_LEN = 32
CMP_STRIDE = 16
CMP_HIDDEN = 256
SLC_BLOCK = 64
SLC_TOPN = 16
WINDOW = 512
D_FF = 4 * D_MODEL
LN_EPS = 1e-5
NEG_INF = -1e30
FORCE_BONUS = 1e4
DEPTH = 1
ALPHA = (2 * DEPTH) ** 0.25

UNSELECTED_BIAS = -float(2 ** 30)
BELOW_NEG_INF = -3e38

SUBLANES = 8
LANES = 128
PROJ_ROWS = 512
CONV_HALO = 32
CONV_ROW_CHUNK = 32
Q_TILE = 256
KEY_CHUNK = 256
WIN_UNIT = 128
CMP_UNIT = 128
LOOP_SPANS = (16, 8, 4, 2)
LOG2_E = 1.4426950408889634
FFN_ROWS = 512
FFN_CHUNK = 1024
VMEM_LIMIT = 56 * 1024 * 1024

RM_COLS = 1792
VT_GROUP_ROWS = 80
VT_ROWS = N_KV * VT_GROUP_ROWS
T_ROWS = 512 + 2 * VT_ROWS + 32


def _layer_norm(z, g, b):
    mu = jnp.mean(z, axis=-1, keepdims=True)
    zc = z - mu
    var = jnp.mean(zc * zc, axis=-1, keepdims=True)
    return zc * lax.rsqrt(var + LN_EPS) * g + b


def _dot(a, b):
    return jnp.dot(a, b, preferred_element_type=F32)


def _dot_nt(a, b):
    return lax.dot_general(a, b, (((1,), (1,)), ((), ())), preferred_element_type=F32)


def _dot_tn(a, b):
    return lax.dot_general(a, b, (((0,), (0,)), ((), ())), preferred_element_type=F32)


def _causal_conv(xs_ref, w_ref, b_ref, g_ref, beta_ref, o_ref):
    n_shift = xs_ref.shape[1]
    x0 = xs_ref[0]
    for b in range(1, SUBLANES):
        xs_ref[b] = pltpu.roll(x0, n_shift - b, axis=0)
    lead = CONV_HALO - (CONV_WIDTH - 1)
    for r in range(0, n_shift - CONV_HALO, CONV_ROW_CHUNK):
        acc = jnp.zeros((CONV_ROW_CHUNK, CONV_CH), F32)
        for k in range(CONV_WIDTH):
            a, b = divmod(lead + k, SUBLANES)
            row = r + a * SUBLANES
            w_k = jnp.concatenate([w_ref[k]] * (CONV_ROW_CHUNK // SUBLANES), axis=0)
            acc = acc + xs_ref[b, row:row + CONV_ROW_CHUNK, :] * w_k
        y = _layer_norm(acc + b_ref[...], g_ref[...], beta_ref[...])
        o_ref[0, r:r + CONV_ROW_CHUNK, :] = (y * jax.nn.sigmoid(y)).astype(BF16)


def _proj_kernel(x_ref, wrm_ref, brm_ref, wt_ref, bt_ref, cw_ref, cb_ref, cg_ref, cbeta_ref,
                 yc_ref, kc_ref, vc_ref, ks_ref, kw_ref,
                 qT_ref, vsT_ref, vwT_ref, gT_ref, xs_ref, carry_ref):
    xb = x_ref[0].astype(BF16)

    def rm(c0, c1):
        return _dot(xb, wrm_ref[:, c0:c1]) + brm_ref[:, c0:c1]

    first_tile = pl.program_id(1) == 0
    xs_ref[0, 0:CONV_HALO, :] = jnp.where(first_tile, 0.0, carry_ref[...])
    half = CONV_CH // 2
    for c in range(0, CONV_CH, half):
        a = rm(c, c + half)
        g = rm(CONV_CH + c, CONV_CH + c + half)
        xs_ref[0, CONV_HALO:, c:c + half] = a * jax.nn.sigmoid(g)
    carry_ref[...] = xs_ref[0, PROJ_ROWS:PROJ_ROWS + CONV_HALO, :]

    _causal_conv(xs_ref, cw_ref, cb_ref, cg_ref, cbeta_ref, yc_ref)

    kv = rm(1024, 1280)
    kc_ref[0, 0] = kv[:, 0:64]
    kc_ref[0, 1] = kv[:, 64:128]
    vc_ref[0, 0] = kv[:, 128:192]
    vc_ref[0, 1] = kv[:, 192:256]

    kz = rm(1280, 1792).astype(BF16)
    ks_ref[0, 0] = kz[:, 0:128]
    ks_ref[0, 1] = kz[:, 128:256]
    kw_ref[0, 0] = kz[:, 256:384]
    kw_ref[0, 1] = kz[:, 384:512]

    def tr(r0, r1):
        return _dot_nt(wt_ref[r0:r1, :], xb) + bt_ref[r0:r1, :]

    qT_ref[0] = tr(0, 512).astype(BF16)
    r = 512
    vs = tr(r, r + VT_ROWS).astype(BF16)
    for u in range(PROJ_ROWS // KEY_CHUNK):
        vsT_ref[0, u] = vs[:, u * KEY_CHUNK:(u + 1) * KEY_CHUNK]
    r += VT_ROWS
    vw = tr(r, r + VT_ROWS).astype(BF16)
    for u in range(PROJ_ROWS // WIN_UNIT):
        vwT_ref[0, u] = vw[:, u * WIN_UNIT:(u + 1) * WIN_UNIT]
    r += VT_ROWS
    gT_ref[0] = jax.nn.sigmoid(tr(r, r + 32))


def _proj_call(x, wrm, brm, wt, bt, conv_w, conv_b, conv_g, conv_beta):
    B, S, D = x.shape
    tm = PROJ_ROWS
    n = S // tm
    const = lambda b, i: (0, 0)
    out_shape = (
        jax.ShapeDtypeStruct((B, S, CONV_CH), BF16),
        jax.ShapeDtypeStruct((B, N_KV, S, HEAD_DIM), F32),
        jax.ShapeDtypeStruct((B, N_KV, S, HEAD_DIM), F32),
        jax.ShapeDtypeStruct((B, N_KV, S, LANES), BF16),
        jax.ShapeDtypeStruct((B, N_KV, S, LANES), BF16),
        jax.ShapeDtypeStruct((B, 512, S), BF16),
        jax.ShapeDtypeStruct((B, S // KEY_CHUNK, VT_ROWS, KEY_CHUNK), BF16),
        jax.ShapeDtypeStruct((B, S // WIN_UNIT, VT_ROWS, WIN_UNIT), BF16),
        jax.ShapeDtypeStruct((B, 32, S), F32),
    )
    kvspec = pl.BlockSpec((1, N_KV, tm, HEAD_DIM), lambda b, i: (b, 0, i, 0))
    kzspec = pl.BlockSpec((1, N_KV, tm, LANES), lambda b, i: (b, 0, i, 0))
    out_specs = (
        pl.BlockSpec((1, tm, CONV_CH), lambda b, i: (b, i, 0)),
        kvspec, kvspec, kzspec, kzspec,
        pl.BlockSpec((1, 512, tm), lambda b, i: (b, 0, i)),
        pl.BlockSpec((1, tm // KEY_CHUNK, VT_ROWS, KEY_CHUNK), lambda b, i: (b, i, 0, 0)),
        pl.BlockSpec((1, tm // WIN_UNIT, VT_ROWS, WIN_UNIT), lambda b, i: (b, i, 0, 0)),
        pl.BlockSpec((1, 32, tm), lambda b, i: (b, 0, i)),
    )
    in_specs = [
        pl.BlockSpec((1, tm, D), lambda b, i: (b, i, 0)),
        pl.BlockSpec((D, RM_COLS), const),
        pl.BlockSpec((1, RM_COLS), const),
        pl.BlockSpec((T_ROWS, D), const),
        pl.BlockSpec((T_ROWS, 1), const),
        pl.BlockSpec((CONV_WIDTH, SUBLANES, CONV_CH), lambda b, i: (0, 0, 0)),
        pl.BlockSpec((1, CONV_CH), const), pl.BlockSpec((1, CONV_CH), const),
        pl.BlockSpec((1, CONV_CH), const),
    ]
    return pl.pallas_call(
        _proj_kernel, grid=(B, n), in_specs=in_specs, out_specs=out_specs,
        out_shape=out_shape, name="proj",
        compiler_params=pltpu.CompilerParams(
            dimension_semantics=("parallel", "arbitrary"), vmem_limit_bytes=VMEM_LIMIT),
        scratch_shapes=[pltpu.VMEM((SUBLANES, tm + CONV_HALO, CONV_CH), F32),
                        pltpu.VMEM((CONV_HALO, CONV_CH), F32)],
    )(x, wrm, brm, wt, bt, conv_w, conv_b, conv_g, conv_beta)


def _compress_hidden(r, pos_ref, w1_ref):
    half = CMP_STRIDE * HEAD_DIM
    n_rows = r.shape[0]
    a = _dot((r + pos_ref[0:1, :]).astype(BF16), w1_ref[0:half, :])
    b = _dot((r + pos_ref[1:2, :]).astype(BF16), w1_ref[half:2 * half, :])
    h = a + pltpu.roll(b, n_rows - 1, axis=0)
    return (h * jax.nn.sigmoid(h)).astype(BF16)


def _compress_kernel(rk_ref, rv_ref, pk_ref, w1k_ref, w2k_ref, pv_ref, w1v_ref, w2vT_ref,
                     kc_ref, vcT_ref):
    n_cmp = kc_ref.shape[2]

    def stride_rows(ref):
        return jnp.concatenate(
            [ref[0, 0, pl.ds(l, n_cmp, stride=CMP_STRIDE), :] for l in range(CMP_STRIDE)], axis=1)

    hk = _compress_hidden(stride_rows(rk_ref), pk_ref, w1k_ref)
    kc_ref[0, 0] = _dot(hk, w2k_ref[...]).astype(BF16)
    hv = _compress_hidden(stride_rows(rv_ref), pv_ref, w1v_ref)
    vcT_ref[0, 0] = _dot_nt(w2vT_ref[...], hv).astype(BF16)


def _compress_call(rk, rv, pk, w1k, w2k, pv, w1v, w2vT):
    B, G, S, _ = rk.shape
    NC = S // CMP_STRIDE
    W = CMP_STRIDE * HEAD_DIM
    const = lambda b, g: (0, 0)
    rspec = pl.BlockSpec((1, 1, S, HEAD_DIM), lambda b, g: (b, g, 0, 0))
    return pl.pallas_call(
        _compress_kernel, grid=(B, G),
        in_specs=[
            rspec, rspec,
            pl.BlockSpec((2, W), const), pl.BlockSpec((2 * W, CMP_HIDDEN), const),
            pl.BlockSpec((CMP_HIDDEN, LANES), const),
            pl.BlockSpec((2, W), const), pl.BlockSpec((2 * W, CMP_HIDDEN), const),
            pl.BlockSpec((HEAD_DIM, CMP_HIDDEN), const),
        ],
        out_specs=(
            pl.BlockSpec((1, 1, NC, LANES), lambda b, g: (b, g, 0, 0)),
            pl.BlockSpec((1, 1, HEAD_DIM, NC), lambda b, g: (b, g, 0, 0)),
        ),
        out_shape=(
            jax.ShapeDtypeStruct((B, G, NC, LANES), BF16),
            jax.ShapeDtypeStruct((B, G, HEAD_DIM, NC), BF16),
        ),
        name="compress",
        compiler_params=pltpu.CompilerParams(
            dimension_semantics=("parallel", "parallel"), vmem_limit_bytes=VMEM_LIMIT),
    )(rk, rv, pk, w1k, w2k, pv, w1v, w2vT)


def _nsa_kernel(q_ref, g_ref, kc_ref, vcT_ref, ks_ref, vsT_ref, kw_ref, vwT_ref,
                e_ref, kdead_ref, mov_ref, cbias_ref, wbias_ref, tbias_ref,
                o_ref, q2_ref, s_a, s_b, p_a, p_b, acc_ref, oc_ref, imp_ref):
    T = Q_TILE
    C = KEY_CHUNK
    qb = pl.program_id(1)
    t0 = qb * T
    n_cmp = kc_ref.shape[2]
    n_slc = mov_ref.shape[0]
    n_chunks = vsT_ref.shape[1]
    groups = range(N_KV)
    heads = range(HPG)
    cols = [(g, h) for g in groups for h in heads]

    def col(g, h):
        return g * HPG + h

    q4 = q_ref[0]
    spare = jnp.where(lax.broadcasted_iota(jnp.int32, (HEAD_DIM, T), 0) == 0,
                      UNSELECTED_BIAS, 0.0).astype(BF16)
    q1 = [jnp.concatenate([q4[i * HEAD_DIM:(i + 1) * HEAD_DIM, :], spare], axis=0)
          for i in range(len(cols))]
    q1_all = [jnp.concatenate(q1[g * HPG:(g + 1) * HPG], axis=1) for g in groups]
    pos_t = t0 + lax.broadcasted_iota(jnp.int32, (1, T), 1)

    c_off = pl.multiple_of(n_cmp - qb * (T // CMP_STRIDE), T // CMP_STRIDE)

    def all_heads(a):
        return jnp.concatenate([a] * HPG, axis=1)

    def head_cols(a):
        return [a[:, h * T:(h + 1) * T] for h in heads]

    def compressed(rows):
        for g in groups:
            sc = (_dot(kc_ref[0, g, 0:rows, :], q1_all[g])
                  + all_heads(cbias_ref[pl.ds(c_off, rows), :]))
            m_c = jnp.max(sc, axis=0, keepdims=True)
            p_c = jnp.exp2(sc - m_c)
            l_c = jnp.sum(p_c, axis=0, keepdims=True)
            p_c = p_c * jnp.where(m_c > 0.5 * NEG_INF, 1.0 / l_c, 0.0)
            oc_ref[g] = _dot(vcT_ref[0, g, :, 0:rows], p_c.astype(BF16))
            psum = functools.reduce(lambda a, b: a + b, head_cols(p_c))
            mov = mov_ref[:, 0:rows]
            p_hi = psum.astype(BF16)
            rem = psum - p_hi.astype(F32)
            p_mid = rem.astype(BF16)
            p_lo = (rem - p_mid.astype(F32)).astype(BF16)
            imp_ref[g] = _dot(mov, p_hi) + _dot(mov, p_mid) + _dot(mov, p_lo)

    cmp_units = (t0 + T - CMP_LEN) // CMP_STRIDE // CMP_UNIT + 1
    for units in range(1, n_cmp // CMP_UNIT + 1):
        pl.when(cmp_units == units)(functools.partial(compressed, units * CMP_UNIT))

    w_keys = WINDOW + T
    start = pl.multiple_of(jnp.maximum(t0 - WINDOW, 0), WIN_UNIT)
    w_off = pl.multiple_of(WINDOW - (t0 - start), WIN_UNIT)
    u0 = start // WIN_UNIT
    sw_raw = [_dot(kw_ref[0, g, pl.ds(start, w_keys), :], q1_all[g]) for g in groups]

    def exact_zero(tile):
        bits = pltpu.bitcast(tile, jnp.uint32)
        return pltpu.bitcast((bits >> 16) >> 16, F32)

    blk = lax.broadcasted_iota(jnp.int32, (n_slc, T), 0)
    blk_f = blk.astype(F32)
    cur = jnp.right_shift(pos_t, 6)
    forced = (blk == 0) | (blk == cur) | (blk == cur - 1)
    valid_b = (blk * SLC_BLOCK) <= pos_t
    bonus = jnp.where(forced, FORCE_BONUS, 0.0)
    work = [jnp.where(valid_b, imp_ref[g] + bonus, NEG_INF) for g in groups]
    for rnd in range(SLC_TOPN):
        for g in groups:
            mx = jnp.max(work[g], axis=0, keepdims=True)
            if rnd == (g + 1) * SLC_TOPN // (N_KV + 1):
                mx = mx + jnp.tile(exact_zero(sw_raw[g][0:SUBLANES, 0:LANES])[0:1], (1, T // LANES))
            first = jnp.min(jnp.where(work[g] == mx, blk_f, float(n_slc)), axis=0, keepdims=True)
            work[g] = jnp.where(blk_f == first, BELOW_NEG_INF, work[g])
    for g in groups:
        picked = work[g] < 0.5 * BELOW_NEG_INF
        unsel = jnp.where(picked & valid_b, 0.0, UNSELECTED_BIAS).astype(BF16)
        for h in heads:
            q2_ref[col(g, h)] = jnp.concatenate([q1[col(g, h)], unsel], axis=0)

    n_full = t0 // C
    uq = qb - n_full * (C // T)

    def key_operands(c, live):
        c = jnp.minimum(c, n_chunks - 1)
        off = pl.multiple_of(c * C, C)
        e_c = e_ref[pl.ds(off, C), :]
        out = []
        for g in groups:
            ks_c = ks_ref[0, g, pl.ds(off, C), :]
            if live is not None:
                ks_c = jnp.where(live, ks_c, kdead_ref[...])
            out.append(jnp.concatenate([ks_c, e_c], axis=1))
        return out

    def values(c):
        return [vsT_ref[0, c, g * VT_GROUP_ROWS:(g + 1) * VT_GROUP_ROWS, :] for g in groups]

    def scores(i, k2, sbuf, bias=None):
        s = _dot(k2, q2_ref[i])
        if bias is not None:
            s = s + bias
        sbuf[i] = s
        return jnp.max(s, axis=0, keepdims=True)

    def weights(i, sbuf, pbuf, cm, m_old):
        m_new = jnp.maximum(m_old, cm)
        pbuf[i] = jnp.exp2(sbuf[i] - m_new).astype(BF16)
        return m_new, jnp.exp2(m_old - m_new)

    def add_values(i, vt, pbuf, rescale):
        acc_ref[i] = rescale * acc_ref[i] + _dot(vt, pbuf[i])

    k2_t = key_operands(n_full, None)
    k2_0 = key_operands(0, 0 < n_full)
    t_off = pl.multiple_of((C // T - 1 - uq) * T, T)
    m0, cm0, cm_t = [], [], []
    for g, h in cols:
        i = col(g, h)
        cm_t.append(scores(i, k2_t[g], s_a, tbias_ref[pl.ds(t_off, C), :]))
        cm0.append(scores(i, k2_0[g], s_b))
        acc_ref[i] = jnp.zeros((VT_GROUP_ROWS, T), F32)

    scores_done = exact_zero(functools.reduce(lambda a, b: a + b, cm_t + cm0))
    o_w = []
    for g in groups:
        v_win = jnp.concatenate(
            [vwT_ref[0, u0 + u, g * VT_GROUP_ROWS:(g + 1) * VT_GROUP_ROWS, :]
             for u in range(w_keys // WIN_UNIT)], axis=1)
        sw = sw_raw[g] + all_heads(wbias_ref[pl.ds(w_off, w_keys), :])
        m_w = jnp.max(sw, axis=0, keepdims=True)
        if g == N_KV - 1:
            m_w = m_w + all_heads(scores_done)
        p_w = jnp.exp2(sw - m_w).astype(BF16)
        ow = _dot(v_win, p_w)
        o_w += head_cols(ow[0:HEAD_DIM] * (1.0 / ow[HEAD_DIM:HEAD_DIM + 1]))

    for i in range(len(cols)):
        m0.append(weights(i, s_a, p_a, cm_t[i], cm_t[i])[0])

    def pair_step(c0, carry):
        m, scale_pend, c_pend, cm_b = carry
        k2_a = key_operands(c0 + 1, c0 + 1 < n_full)
        k2_b = key_operands(c0 + 2, c0 + 2 < n_full)
        vt_pend = values(c_pend)
        vt_0 = values(c0)
        m1, scale0, cm_a = [], [], []
        for g, h in cols:
            i = col(g, h)
            add_values(i, vt_pend[g], p_a, scale_pend[i])
            cm_a.append(scores(i, k2_a[g], s_a))
            m_i, s_i = weights(i, s_b, p_b, cm_b[i], m[i])
            m1.append(m_i)
            scale0.append(s_i)
        m2, scale1, cm_b2 = [], [], []
        for g, h in cols:
            i = col(g, h)
            cm_b2.append(scores(i, k2_b[g], s_b))
            add_values(i, vt_0[g], p_b, scale0[i])
            m_i, s_i = weights(i, s_a, p_a, cm_a[i], m1[i])
            m2.append(m_i)
            scale1.append(s_i)
        return tuple(m2), tuple(scale1), jnp.minimum(c0 + 1, n_chunks - 1), tuple(cm_b2)

    carry = (tuple(m0), tuple(jnp.ones((1, T), F32) for _ in cols), n_full, tuple(cm0))
    done = 0
    for span in LOOP_SPANS:
        left = n_full - done
        trips = (left + 1) // 2 if span == 2 else left // span

        def trip(t, c, span=span, done=done):
            for j in range(0, span, 2):
                c = pair_step(done + span * t + j, c)
            return c

        carry = lax.fori_loop(0, trips, trip, carry)
        done = done + trips * span
    _, scale_pend, c_pend, _ = carry

    vt_pend = values(c_pend)
    for g, h in cols:
        i = col(g, h)
        add_values(i, vt_pend[g], p_a, scale_pend[i])
        acc = acc_ref[i]
        o_s = acc[0:HEAD_DIM] * (1.0 / acc[HEAD_DIM:HEAD_DIM + 1])
        o_c = oc_ref[g, :, h * T:(h + 1) * T]
        gate = [g_ref[0, pl.ds(i * N_BRANCH + br, 1), :] for br in range(N_BRANCH)]
        out = gate[0] * o_c + gate[1] * o_s + gate[2] * o_w[i]
        o_ref[0, i * HEAD_DIM:(i + 1) * HEAD_DIM, :] = out.astype(BF16)


def _nsa_call(qT, gT, kc, vcT, ks, vsT, kw, vwT, consts):
    B, _, S = qT.shape
    G = N_KV
    T = Q_TILE
    C = KEY_CHUNK
    n_cmp = kc.shape[2]
    rows = G * HPG * HEAD_DIM
    n_cols = G * HPG
    e_mat, kdead, movT, cbias, wbias, tbias = consts
    once = pl.Buffered(1)
    per_batch = lambda shape: pl.BlockSpec((1,) + shape, lambda b, i: (b, 0, 0, 0), pipeline_mode=once)
    full = lambda a: pl.BlockSpec(a.shape, lambda b, i: (0, 0), pipeline_mode=once)
    in_specs = [
        pl.BlockSpec((1, rows, T), lambda b, i: (b, 0, i)),
        pl.BlockSpec((1, 32, T), lambda b, i: (b, 0, i)),
        per_batch((G, n_cmp, LANES)),
        per_batch((G, HEAD_DIM, n_cmp)),
        per_batch((G, S, LANES)),
        per_batch((S // C, VT_ROWS, C)),
        per_batch((G, S, LANES)),
        per_batch((S // WIN_UNIT, VT_ROWS, WIN_UNIT)),
        full(e_mat), full(kdead), full(movT), full(cbias), full(wbias), full(tbias),
    ]
    return pl.pallas_call(
        _nsa_kernel, grid=(B, S // T), in_specs=in_specs,
        out_specs=pl.BlockSpec((1, rows, T), lambda b, i: (b, 0, i)),
        out_shape=jax.ShapeDtypeStruct((B, rows, S), BF16),
        scratch_shapes=[pltpu.VMEM((n_cols, 2 * LANES, T), BF16),
                        pltpu.VMEM((n_cols, C, T), F32), pltpu.VMEM((n_cols, C, T), F32),
                        pltpu.VMEM((n_cols, C, T), BF16), pltpu.VMEM((n_cols, C, T), BF16),
                        pltpu.VMEM((n_cols, VT_GROUP_ROWS, T), F32),
                        pltpu.VMEM((G, HEAD_DIM, HPG * T), F32),
                        pltpu.VMEM((G, S // SLC_BLOCK, T), F32)],
        name="nsa",
        compiler_params=pltpu.CompilerParams(
            dimension_semantics=("parallel", "arbitrary"),
            vmem_limit_bytes=VMEM_LIMIT),
    )(qT, gT, kc, vcT, ks, vsT, kw, vwT, e_mat, kdead, movT, cbias, wbias, tbias)


def _mix_ffn_kernel(yc_ref, ynT_ref, x_ref, p_ref, wc_ref, wn_ref, bo_ref, g1_ref, beta1_ref,
                    wup_ref, bup_ref, wdn_ref, bdn_ref, wpe_ref, wpg_ref, g2_ref, beta2_ref,
                    o_ref, acc_ref):
    mix = _dot(yc_ref[0], wc_ref[...]) + _dot_tn(ynT_ref[0], wn_ref[...])
    x1 = _layer_norm(ALPHA * x_ref[0] + mix + bo_ref[...], g1_ref[...], beta1_ref[...])
    xb = x1.astype(BF16)
    ple = _dot(p_ref[0].astype(BF16), wpe_ref[...]) * jax.nn.sigmoid(_dot(xb, wpg_ref[...]))
    acc_ref[...] = ALPHA * x1 + ple + bdn_ref[...]
    for c in range(0, D_FF, FFN_CHUNK):
        u = _dot(xb, wup_ref[:, c:c + FFN_CHUNK]) + bup_ref[:, c:c + FFN_CHUNK]
        u = jnp.square(jnp.maximum(u, 0.0)).astype(BF16)
        acc_ref[...] += _dot(u, wdn_ref[c:c + FFN_CHUNK, :])
    o_ref[0] = _layer_norm(acc_ref[...], g2_ref[...], beta2_ref[...])


def _mix_ffn_call(yc, ynT, x, p, wc, wn, bo, g1, beta1, wup, bup, wdn, bdn, wpe, wpg, g2, beta2):
    B, S, D = x.shape
    tm = FFN_ROWS
    const = lambda b, i: (0, 0)
    rows = lambda width: pl.BlockSpec((1, tm, width), lambda b, i: (b, i, 0))
    weight = lambda shape: pl.BlockSpec(shape, const, pipeline_mode=pl.Buffered(1))
    vec = pl.BlockSpec((1, D), const)
    return pl.pallas_call(
        _mix_ffn_kernel, grid=(B, S // tm),
        in_specs=[
            rows(CONV_CH),
            pl.BlockSpec((1, D - CONV_CH, tm), lambda b, i: (b, 0, i)),
            rows(D), rows(PLE_DIM),
            weight((CONV_CH, D)), weight((D - CONV_CH, D)), vec, vec, vec,
            weight((D, D_FF)), pl.BlockSpec((1, D_FF), const), weight((D_FF, D)), vec,
            weight((PLE_DIM, D)), weight((D, D)), vec, vec,
        ],
        out_specs=rows(D),
        out_shape=jax.ShapeDtypeStruct((B, S, D), F32),
        scratch_shapes=[pltpu.VMEM((tm, D), F32)],
        name="mixffn",
        compiler_params=pltpu.CompilerParams(
            dimension_semantics=("parallel", "parallel"), vmem_limit_bytes=VMEM_LIMIT),
    )(yc, ynT, x, p, wc, wn, bo, g1, beta1, wup, bup, wdn, bdn, wpe, wpg, g2, beta2)


def _pad_groups(w):
    z = jnp.zeros(w.shape[:-1] + (HEAD_DIM,), w.dtype)
    return jnp.concatenate([w[..., :HEAD_DIM], z, w[..., HEAD_DIM:], z], axis=-1)


def _overlap_matrix_t(n_cmp_padded, n_slc):
    n_cmp = n_cmp_padded - 1
    c0 = np.arange(n_cmp) * CMP_STRIDE
    c1 = c0 + CMP_LEN - 1
    s0 = np.arange(n_slc) * SLC_BLOCK
    s1 = s0 + SLC_BLOCK - 1
    m = ((c0[:, None] <= s1[None, :]) & (c1[:, None] >= s0[None, :])).astype(np.float32)
    out = np.zeros((n_slc, n_cmp_padded), np.float32)
    out[:, :n_cmp] = m.T
    return out


def _vt_rows(w, bias):
    d = w.shape[0]
    pad = VT_GROUP_ROWS - HEAD_DIM
    one = jnp.zeros((pad,), F32).at[0].set(1.0)
    ws, bs = [], []
    for g in range(N_KV):
        ws += [w[:, g * HEAD_DIM:(g + 1) * HEAD_DIM], jnp.zeros((d, pad), F32)]
        bs += [bias[g * HEAD_DIM:(g + 1) * HEAD_DIM], one]
    return jnp.concatenate(ws, axis=1), jnp.concatenate(bs)


def _nsa_constants(S, n_cmp_p):
    T, C = Q_TILE, KEY_CHUNK
    t = np.arange(T)[None, :]
    neg = np.float32(NEG_INF)

    def bias(valid):
        return jnp.asarray(np.where(valid, np.float32(0.0), neg))

    e_mat = (np.arange(S)[:, None] // SLC_BLOCK == np.arange(S // SLC_BLOCK)[None, :])
    kdead = np.zeros((C, LANES), np.float32)
    kdead[:, HEAD_DIM] = 1.0
    j = np.arange(2 * n_cmp_p)[:, None]
    cbias = bias(CMP_STRIDE * (j - n_cmp_p) + CMP_LEN - 1 <= t)
    j = np.arange(2 * WINDOW + T)[:, None]
    wbias = bias((t < j) & (j <= WINDOW + t))
    j = np.arange(2 * C - T)[:, None]
    tbias = bias(j - (C - T) <= t)
    return (jnp.asarray(e_mat.astype(np.float32), BF16), jnp.asarray(kdead, BF16),
            jnp.asarray(_overlap_matrix_t(n_cmp_p, S // SLC_BLOCK), BF16), cbias, wbias, tbias)


def _layer(x, p, w_in, b_in, conv_dw_w, conv_dw_b, conv_ln_g, conv_ln_b,
           cmp_pos_k, cmp_w1_k, cmp_w2_k, cmp_pos_v, cmp_w1_v, cmp_w2_v,
           w_out, b_out, ln1_g, ln1_b, w_up, b_up, w_down, b_down, w_pe, w_pg, ln2_g, ln2_b):
    B, S, D = x.shape
    scale = HEAD_DIM ** -0.5 * LOG2_E
    row = lambda v: v.reshape(1, -1).astype(F32)

    w, bias = w_in, b_in
    wrm = jnp.concatenate(
        [w[:, 0:1024], w[:, 1536:1792], _pad_groups(w[:, 1792:1920]), _pad_groups(w[:, 2048:2176])],
        axis=1).astype(BF16)
    brm = row(jnp.concatenate(
        [bias[0:1024], bias[1536:1792], _pad_groups(bias[1792:1920]), _pad_groups(bias[2048:2176])]))
    w_vs, b_vs = _vt_rows(w[:, 1920:2048], bias[1920:2048])
    w_vw, b_vw = _vt_rows(w[:, 2176:2304], bias[2176:2304])
    wt = jnp.concatenate(
        [w[:, 1024:1536] * scale, w_vs, w_vw, w[:, 2304:2328], jnp.zeros((D, 8), F32)],
        axis=1).T.astype(BF16)
    bt = jnp.concatenate(
        [bias[1024:1536] * scale, b_vs, b_vw, bias[2304:2328], jnp.zeros((8,), F32)]).reshape(-1, 1)

    conv_w = jnp.broadcast_to(conv_dw_w.reshape(CONV_WIDTH, 1, CONV_CH),
                              (CONV_WIDTH, SUBLANES, CONV_CH))
    y_conv, kc, vc, ks, kw, qT, vsT, vwT, gT = _proj_call(
        x, wrm, brm, wt, bt, conv_w, row(conv_dw_b), row(conv_ln_g), row(conv_ln_b))

    n_cmp_p = S // CMP_STRIDE
    blk_w = CMP_STRIDE * HEAD_DIM
    w2k = jnp.concatenate([cmp_w2_k, jnp.zeros((CMP_HIDDEN, LANES - HEAD_DIM), F32)], axis=1)
    kcz, vcT = _compress_call(
        kc, vc, cmp_pos_k.reshape(2, blk_w), cmp_w1_k.astype(BF16), w2k.astype(BF16),
        cmp_pos_v.reshape(2, blk_w), cmp_w1_v.astype(BF16), cmp_w2_v.T.astype(BF16))

    y_nsaT = _nsa_call(qT, gT, kcz, vcT, ks, vsT, kw, vwT, _nsa_constants(S, n_cmp_p))

    wo = w_out.astype(BF16)
    return _mix_ffn_call(
        y_conv, y_nsaT, x, p, wo[:CONV_CH], wo[CONV_CH:], row(b_out), row(ln1_g), row(ln1_b),
        w_up.astype(BF16), row(b_up), w_down.astype(BF16), row(b_down),
        w_pe.astype(BF16), w_pg.astype(BF16), row(ln2_g), row(ln2_b))


def kernel(x, p, w_in, b_in, conv_dw_w, conv_dw_b, conv_ln_g, conv_ln_b, cmp_pos_k, cmp_w1_k, cmp_w2_k, cmp_pos_v, cmp_w1_v, cmp_w2_v, w_out, b_out, ln1_g, ln1_b, w_up, b_up, w_down, b_down, w_pe, w_pg, ln2_g, ln2_b):
    params = (w_in, b_in, conv_dw_w, conv_dw_b, conv_ln_g, conv_ln_b,
              cmp_pos_k, cmp_w1_k, cmp_w2_k, cmp_pos_v, cmp_w1_v, cmp_w2_v,
              w_out, b_out, ln1_g, ln1_b, w_up, b_up, w_down, b_down, w_pe, w_pg, ln2_g, ln2_b)
    for i in range(DEPTH):
        x = _layer(x, p[i], *[t[i] for t in params])
    return x
```

```python
import functools

import jax
import jax.numpy as jnp
import numpy as np
from jax import lax
from jax.experimental import pallas as pl
from jax.experimental.pallas import tpu as pltpu

F32 = jnp.float32
BF16 = jnp.bfloat16

D_MODEL = 1024
PLE_DIM = 256
CONV_CH = 512
CONV_WIDTH = 31
HEAD_DIM = 64
N_KV = 2
HPG = 4
N_BRANCH = 3
CMP_LEN = 32
CMP_STRIDE = 16
CMP_HIDDEN = 256
SLC_BLOCK = 64
SLC_TOPN = 16
WINDOW = 512
D_FF = 4 * D_MODEL
LN_EPS = 1e-5
NEG_INF = -1e30
FORCE_BONUS = 1e4
DEPTH = 1
ALPHA = (2 * DEPTH) ** 0.25

UNSELECTED_BIAS = -float(2 ** 30)
BELOW_NEG_INF = -3e38

SUBLANES = 8
LANES = 128
PROJ_ROWS = 512
CONV_HALO = 32
CONV_ROW_CHUNK = 32
Q_TILE = 256
KEY_CHUNK = 256
WIN_UNIT = 128
CMP_UNIT = 128
LOOP_SPANS = (8, 4, 2)
LOG2_E = 1.4426950408889634
FFN_ROWS = 512
FFN_CHUNK = 1024
VMEM_LIMIT = 56 * 1024 * 1024

RM_COLS = 1792
VT_GROUP_ROWS = 80
VT_ROWS = N_KV * VT_GROUP_ROWS
T_ROWS = 512 + 2 * VT_ROWS + 32


def _layer_norm(z, g, b):
    mu = jnp.mean(z, axis=-1, keepdims=True)
    zc = z - mu
    var = jnp.mean(zc * zc, axis=-1, keepdims=True)
    return zc * lax.rsqrt(var + LN_EPS) * g + b


def _dot(a, b):
    return jnp.dot(a, b, preferred_element_type=F32)


def _dot_nt(a, b):
    return lax.dot_general(a, b, (((1,), (1,)), ((), ())), preferred_element_type=F32)


def _dot_tn(a, b):
    return lax.dot_general(a, b, (((0,), (0,)), ((), ())), preferred_element_type=F32)


def _causal_conv(xs_ref, w_ref, b_ref, g_ref, beta_ref, o_ref):
    n_shift = xs_ref.shape[1]
    x0 = xs_ref[0]
    for b in range(1, SUBLANES):
        xs_ref[b] = pltpu.roll(x0, n_shift - b, axis=0)
    lead = CONV_HALO - (CONV_WIDTH - 1)
    for r in range(0, n_shift - CONV_HALO, CONV_ROW_CHUNK):
        acc = jnp.zeros((CONV_ROW_CHUNK, CONV_CH), F32)
        for k in range(CONV_WIDTH):
            a, b = divmod(lead + k, SUBLANES)
            row = r + a * SUBLANES
            w_k = jnp.concatenate([w_ref[k]] * (CONV_ROW_CHUNK // SUBLANES), axis=0)
            acc = acc + xs_ref[b, row:row + CONV_ROW_CHUNK, :] * w_k
        y = _layer_norm(acc + b_ref[...], g_ref[...], beta_ref[...])
        o_ref[0, r:r + CONV_ROW_CHUNK, :] = (y * jax.nn.sigmoid(y)).astype(BF16)


def _proj_kernel(x_ref, wrm_ref, brm_ref, wt_ref, bt_ref, cw_ref, cb_ref, cg_ref, cbeta_ref,
                 yc_ref, kc_ref, vc_ref, ks_ref, kw_ref,
                 qT_ref, vsT_ref, vwT_ref, gT_ref, xs_ref, carry_ref):
    xb = x_ref[0].astype(BF16)

    def rm(c0, c1):
        return _dot(xb, wrm_ref[:, c0:c1]) + brm_ref[:, c0:c1]

    first_tile = pl.program_id(1) == 0
    xs_ref[0, 0:CONV_HALO, :] = jnp.where(first_tile, 0.0, carry_ref[...])
    half = CONV_CH // 2
    for c in range(0, CONV_CH, half):
        a = rm(c, c + half)
        g = rm(CONV_CH + c, CONV_CH + c + half)
        xs_ref[0, CONV_HALO:, c:c + half] = a * jax.nn.sigmoid(g)
    carry_ref[...] = xs_ref[0, PROJ_ROWS:PROJ_ROWS + CONV_HALO, :]

    _causal_conv(xs_ref, cw_ref, cb_ref, cg_ref, cbeta_ref, yc_ref)

    kv = rm(1024, 1280)
    kc_ref[0, 0] = kv[:, 0:64]
    kc_ref[0, 1] = kv[:, 64:128]
    vc_ref[0, 0] = kv[:, 128:192]
    vc_ref[0, 1] = kv[:, 192:256]

    kz = rm(1280, 1792).astype(BF16)
    ks_ref[0, 0] = kz[:, 0:128]
    ks_ref[0, 1] = kz[:, 128:256]
    kw_ref[0, 0] = kz[:, 256:384]
    kw_ref[0, 1] = kz[:, 384:512]

    def tr(r0, r1):
        return _dot_nt(wt_ref[r0:r1, :], xb) + bt_ref[r0:r1, :]

    qT_ref[0] = tr(0, 512).astype(BF16)
    r = 512
    vs = tr(r, r + VT_ROWS).astype(BF16)
    for u in range(PROJ_ROWS // KEY_CHUNK):
        vsT_ref[0, u] = vs[:, u * KEY_CHUNK:(u + 1) * KEY_CHUNK]
    r += VT_ROWS
    vw = tr(r, r + VT_ROWS).astype(BF16)
    for u in range(PROJ_ROWS // WIN_UNIT):
        vwT_ref[0, u] = vw[:, u * WIN_UNIT:(u + 1) * WIN_UNIT]
    r += VT_ROWS
    gT_ref[0] = jax.nn.sigmoid(tr(r, r + 32))


def _proj_call(x, wrm, brm, wt, bt, conv_w, conv_b, conv_g, conv_beta):
    B, S, D = x.shape
    tm = PROJ_ROWS
    n = S // tm
    const = lambda b, i: (0, 0)
    out_shape = (
        jax.ShapeDtypeStruct((B, S, CONV_CH), BF16),
        jax.ShapeDtypeStruct((B, N_KV, S, HEAD_DIM), F32),
        jax.ShapeDtypeStruct((B, N_KV, S, HEAD_DIM), F32),
        jax.ShapeDtypeStruct((B, N_KV, S, LANES), BF16),
        jax.ShapeDtypeStruct((B, N_KV, S, LANES), BF16),
        jax.ShapeDtypeStruct((B, 512, S), BF16),
        jax.ShapeDtypeStruct((B, S // KEY_CHUNK, VT_ROWS, KEY_CHUNK), BF16),
        jax.ShapeDtypeStruct((B, S // WIN_UNIT, VT_ROWS, WIN_UNIT), BF16),
        jax.ShapeDtypeStruct((B, 32, S), F32),
    )
    kvspec = pl.BlockSpec((1, N_KV, tm, HEAD_DIM), lambda b, i: (b, 0, i, 0))
    kzspec = pl.BlockSpec((1, N_KV, tm, LANES), lambda b, i: (b, 0, i, 0))
    out_specs = (
        pl.BlockSpec((1, tm, CONV_CH), lambda b, i: (b, i, 0)),
        kvspec, kvspec, kzspec, kzspec,
        pl.BlockSpec((1, 512, tm), lambda b, i: (b, 0, i)),
        pl.BlockSpec((1, tm // KEY_CHUNK, VT_ROWS, KEY_CHUNK), lambda b, i: (b, i, 0, 0)),
        pl.BlockSpec((1, tm // WIN_UNIT, VT_ROWS, WIN_UNIT), lambda b, i: (b, i, 0, 0)),
        pl.BlockSpec((1, 32, tm), lambda b, i: (b, 0, i)),
    )
    in_specs = [
        pl.BlockSpec((1, tm, D), lambda b, i: (b, i, 0)),
        pl.BlockSpec((D, RM_COLS), const),
        pl.BlockSpec((1, RM_COLS), const),
        pl.BlockSpec((T_ROWS, D), const),
        pl.BlockSpec((T_ROWS, 1), const),
        pl.BlockSpec((CONV_WIDTH, SUBLANES, CONV_CH), lambda b, i: (0, 0, 0)),
        pl.BlockSpec((1, CONV_CH), const), pl.BlockSpec((1, CONV_CH), const),
        pl.BlockSpec((1, CONV_CH), const),
    ]
    return pl.pallas_call(
        _proj_kernel, grid=(B, n), in_specs=in_specs, out_specs=out_specs,
        out_shape=out_shape, name="proj",
        compiler_params=pltpu.CompilerParams(
            dimension_semantics=("parallel", "arbitrary"), vmem_limit_bytes=VMEM_LIMIT),
        scratch_shapes=[pltpu.VMEM((SUBLANES, tm + CONV_HALO, CONV_CH), F32),
                        pltpu.VMEM((CONV_HALO, CONV_CH), F32)],
    )(x, wrm, brm, wt, bt, conv_w, conv_b, conv_g, conv_beta)


def _compress_hidden(r, pos_ref, w1_ref):
    half = CMP_STRIDE * HEAD_DIM
    n_rows = r.shape[0]
    a = _dot((r + pos_ref[0:1, :]).astype(BF16), w1_ref[0:half, :])
    b = _dot((r + pos_ref[1:2, :]).astype(BF16), w1_ref[half:2 * half, :])
    h = a + pltpu.roll(b, n_rows - 1, axis=0)
    return (h * jax.nn.sigmoid(h)).astype(BF16)


def _compress_kernel(rk_ref, rv_ref, pk_ref, w1k_ref, w2k_ref, pv_ref, w1v_ref, w2vT_ref,
                     kc_ref, vcT_ref):
    n_cmp = kc_ref.shape[2]

    def stride_rows(ref):
        return jnp.concatenate(
            [ref[0, 0, pl.ds(l, n_cmp, stride=CMP_STRIDE), :] for l in range(CMP_STRIDE)], axis=1)

    hk = _compress_hidden(stride_rows(rk_ref), pk_ref, w1k_ref)
    kc_ref[0, 0] = _dot(hk, w2k_ref[...]).astype(BF16)
    hv = _compress_hidden(stride_rows(rv_ref), pv_ref, w1v_ref)
    vcT_ref[0, 0] = _dot_nt(w2vT_ref[...], hv).astype(BF16)


def _compress_call(rk, rv, pk, w1k, w2k, pv, w1v, w2vT):
    B, G, S, _ = rk.shape
    NC = S // CMP_STRIDE
    W = CMP_STRIDE * HEAD_DIM
    const = lambda b, g: (0, 0)
    rspec = pl.BlockSpec((1, 1, S, HEAD_DIM), lambda b, g: (b, g, 0, 0))
    return pl.pallas_call(
        _compress_kernel, grid=(B, G),
        in_specs=[
            rspec, rspec,
            pl.BlockSpec((2, W), const), pl.BlockSpec((2 * W, CMP_HIDDEN), const),
            pl.BlockSpec((CMP_HIDDEN, LANES), const),
            pl.BlockSpec((2, W), const), pl.BlockSpec((2 * W, CMP_HIDDEN), const),
            pl.BlockSpec((HEAD_DIM, CMP_HIDDEN), const),
        ],
        out_specs=(
            pl.BlockSpec((1, 1, NC, LANES), lambda b, g: (b, g, 0, 0)),
            pl.BlockSpec((1, 1, HEAD_DIM, NC), lambda b, g: (b, g, 0, 0)),
        ),
        out_shape=(
            jax.ShapeDtypeStruct((B, G, NC, LANES), BF16),
            jax.ShapeDtypeStruct((B, G, HEAD_DIM, NC), BF16),
        ),
        name="compress",
        compiler_params=pltpu.CompilerParams(
            dimension_semantics=("parallel", "parallel"), vmem_limit_bytes=VMEM_LIMIT),
    )(rk, rv, pk, w1k, w2k, pv, w1v, w2vT)


def _nsa_kernel(q_ref, g_ref, kc_ref, vcT_ref, ks_ref, vsT_ref, kw_ref, vwT_ref,
                e_ref, kdead_ref, mov_ref, cbias_ref, wbias_ref, tbias_ref,
                o_ref, q2_ref, s_a, s_b, p_a, p_b, acc_ref, oc_ref, imp_ref):
    T = Q_TILE
    C = KEY_CHUNK
    qb = pl.program_id(1)
    t0 = qb * T
    n_cmp = kc_ref.shape[2]
    n_slc = mov_ref.shape[0]
    n_chunks = vsT_ref.shape[1]
    groups = range(N_KV)
    heads = range(HPG)
    cols = [(g, h) for g in groups for h in heads]

    def col(g, h):
        return g * HPG + h

    q4 = q_ref[0]
    spare = jnp.where(lax.broadcasted_iota(jnp.int32, (HEAD_DIM, T), 0) == 0,
                      UNSELECTED_BIAS, 0.0).astype(BF16)
    q1 = [jnp.concatenate([q4[i * HEAD_DIM:(i + 1) * HEAD_DIM, :], spare], axis=0)
          for i in range(len(cols))]
    q1_all = [jnp.concatenate(q1[g * HPG:(g + 1) * HPG], axis=1) for g in groups]
    pos_t = t0 + lax.broadcasted_iota(jnp.int32, (1, T), 1)

    c_off = pl.multiple_of(n_cmp - qb * (T // CMP_STRIDE), T // CMP_STRIDE)

    def all_heads(a):
        return jnp.concatenate([a] * HPG, axis=1)

    def head_cols(a):
        return [a[:, h * T:(h + 1) * T] for h in heads]

    def compressed(rows):
        for g in groups:
            sc = (_dot(kc_ref[0, g, 0:rows, :], q1_all[g])
                  + all_heads(cbias_ref[pl.ds(c_off, rows), :]))
            m_c = jnp.max(sc, axis=0, keepdims=True)
            p_c = jnp.exp2(sc - m_c)
            l_c = jnp.sum(p_c, axis=0, keepdims=True)
            p_c = p_c * jnp.where(m_c > 0.5 * NEG_INF, 1.0 / l_c, 0.0)
            oc_ref[g] = _dot(vcT_ref[0, g, :, 0:rows], p_c.astype(BF16))
            psum = functools.reduce(lambda a, b: a + b, head_cols(p_c))
            mov = mov_ref[:, 0:rows]
            p_hi = psum.astype(BF16)
            rem = psum - p_hi.astype(F32)
            p_mid = rem.astype(BF16)
            p_lo = (rem - p_mid.astype(F32)).astype(BF16)
            imp_ref[g] = _dot(mov, p_hi) + _dot(mov, p_mid) + _dot(mov, p_lo)

    cmp_units = (t0 + T - CMP_LEN) // CMP_STRIDE // CMP_UNIT + 1
    for units in range(1, n_cmp // CMP_UNIT + 1):
        pl.when(cmp_units == units)(functools.partial(compressed, units * CMP_UNIT))

    TH = T // 2
    w_keys = WINDOW + TH
    start = pl.multiple_of(jnp.maximum(t0 - WINDOW, 0), WIN_UNIT)
    w_off = pl.multiple_of(WINDOW - (t0 - start), WIN_UNIT)
    w_row = [pl.multiple_of(jnp.maximum(qh * TH - w_off, 0), WIN_UNIT) for qh in range(2)]

    def window_queries(g, qh):
        return jnp.concatenate([q1[col(g, h)][:, qh * TH:(qh + 1) * TH] for h in heads], axis=1)

    sw_raw = [[_dot(kw_ref[0, g, pl.ds(start + w_row[qh], w_keys), :], window_queries(g, qh))
               for qh in range(2)] for g in groups]

    def exact_zero(tile):
        bits = pltpu.bitcast(tile, jnp.uint32)
        return pltpu.bitcast((bits >> 16) >> 16, F32)

    blk = lax.broadcasted_iota(jnp.int32, (n_slc, T), 0)
    blk_f = blk.astype(F32)
    cur = jnp.right_shift(pos_t, 6)
    forced = (blk == 0) | (blk == cur) | (blk == cur - 1)
    valid_b = (blk * SLC_BLOCK) <= pos_t
    bonus = jnp.where(forced, FORCE_BONUS, 0.0)
    work = [jnp.where(valid_b, imp_ref[g] + bonus, NEG_INF) for g in groups]
    for rnd in range(SLC_TOPN):
        for g in groups:
            mx = jnp.max(work[g], axis=0, keepdims=True)
            if rnd == (g + 1) * SLC_TOPN // (N_KV + 1):
                tiles = sw_raw[g][0][0:SUBLANES, 0:LANES] + sw_raw[g][1][0:SUBLANES, 0:LANES]
                mx = mx + jnp.tile(exact_zero(tiles)[0:1], (1, T // LANES))
            first = jnp.min(jnp.where(work[g] == mx, blk_f, float(n_slc)), axis=0, keepdims=True)
            work[g] = jnp.where(blk_f == first, BELOW_NEG_INF, work[g])
    for g in groups:
        picked = work[g] < 0.5 * BELOW_NEG_INF
        unsel = jnp.where(picked & valid_b, 0.0, UNSELECTED_BIAS).astype(BF16)
        for h in heads:
            q2_ref[col(g, h)] = jnp.concatenate([q1[col(g, h)], unsel], axis=0)

    n_full = t0 // C
    uq = qb - n_full * (C // T)

    def key_operands(c, live):
        c = jnp.minimum(c, n_chunks - 1)
        off = pl.multiple_of(c * C, C)
        e_c = e_ref[pl.ds(off, C), :]
        out = []
        for g in groups:
            ks_c = ks_ref[0, g, pl.ds(off, C), :]
            if live is not None:
                ks_c = jnp.where(live, ks_c, kdead_ref[...])
            out.append(jnp.concatenate([ks_c, e_c], axis=1))
        return out

    def values(c):
        return [vsT_ref[0, c, g * VT_GROUP_ROWS:(g + 1) * VT_GROUP_ROWS, :] for g in groups]

    def scores(i, k2, sbuf, bias=None):
        s = _dot(k2, q2_ref[i])
        if bias is not None:
            s = s + bias
        sbuf[i] = s
        return jnp.max(s, axis=0, keepdims=True)

    def weights(i, sbuf, pbuf, cm, m_old):
        m_new = jnp.maximum(m_old, cm)
        pbuf[i] = jnp.exp2(sbuf[i] - m_new).astype(BF16)
        return m_new, jnp.exp2(m_old - m_new)

    def add_values(i, vt, pbuf, rescale):
        acc_ref[i] = rescale * acc_ref[i] + _dot(vt, pbuf[i])

    k2_t = key_operands(n_full, None)
    k2_0 = key_operands(0, 0 < n_full)
    t_off = pl.multiple_of((C // T - 1 - uq) * T, T)
    m0, cm0, cm_t = [], [], []
    for g, h in cols:
        i = col(g, h)
        cm_t.append(scores(i, k2_t[g], s_a, tbias_ref[pl.ds(t_off, C), :]))
        cm0.append(scores(i, k2_0[g], s_b))
        acc_ref[i] = jnp.zeros((VT_GROUP_ROWS, T), F32)

    scores_done = exact_zero(functools.reduce(lambda a, b: a + b, cm_t + cm0))
    o_w = []
    for g in groups:
        halves = []
        for qh in range(2):
            u0 = (start + w_row[qh]) // WIN_UNIT
            v_win = jnp.concatenate(
                [vwT_ref[0, u0 + u, g * VT_GROUP_ROWS:(g + 1) * VT_GROUP_ROWS, :]
                 for u in range(w_keys // WIN_UNIT)], axis=1)
            bias = wbias_ref[pl.ds(w_off + w_row[qh], w_keys), qh * TH:(qh + 1) * TH]
            sw = sw_raw[g][qh] + all_heads(bias)
            m_w = jnp.max(sw, axis=0, keepdims=True)
            if g == N_KV - 1:
                m_w = m_w + all_heads(scores_done[:, 0:TH])
            p_w = jnp.exp2(sw - m_w).astype(BF16)
            ow = _dot(v_win, p_w)
            halves.append(ow[0:HEAD_DIM] * (1.0 / ow[HEAD_DIM:HEAD_DIM + 1]))
        o_w += [jnp.concatenate([halves[0][:, h * TH:(h + 1) * TH],
                                 halves[1][:, h * TH:(h + 1) * TH]], axis=1) for h in heads]

    for i in range(len(cols)):
        m0.append(weights(i, s_a, p_a, cm_t[i], cm_t[i])[0])

    def pair_step(c0, carry):
        m, scale_pend, c_pend, cm_b = carry
        k2_a = key_operands(c0 + 1, c0 + 1 < n_full)
        k2_b = key_operands(c0 + 2, c0 + 2 < n_full)
        vt_pend = values(c_pend)
        vt_0 = values(c0)
        m1, scale0, cm_a = [], [], []
        for g, h in cols:
            i = col(g, h)
            add_values(i, vt_pend[g], p_a, scale_pend[i])
            cm_a.append(scores(i, k2_a[g], s_a))
            m_i, s_i = weights(i, s_b, p_b, cm_b[i], m[i])
            m1.append(m_i)
            scale0.append(s_i)
        m2, scale1, cm_b2 = [], [], []
        for g, h in cols:
            i = col(g, h)
            cm_b2.append(scores(i, k2_b[g], s_b))
            add_values(i, vt_0[g], p_b, scale0[i])
            m_i, s_i = weights(i, s_a, p_a, cm_a[i], m1[i])
            m2.append(m_i)
            scale1.append(s_i)
        return tuple(m2), tuple(scale1), jnp.minimum(c0 + 1, n_chunks - 1), tuple(cm_b2)

    carry = (tuple(m0), tuple(jnp.ones((1, T), F32) for _ in cols), n_full, tuple(cm0))
    done = 0
    for span in LOOP_SPANS:
        left = n_full - done
        trips = (left + 1) // 2 if span == 2 else left // span

        def trip(t, c, span=span, done=done):
            for j in range(0, span, 2):
                c = pair_step(done + span * t + j, c)
            return c

        carry = lax.fori_loop(0, trips, trip, carry)
        done = done + trips * span
    _, scale_pend, c_pend, _ = carry

    vt_pend = values(c_pend)
    for g, h in cols:
        i = col(g, h)
        add_values(i, vt_pend[g], p_a, scale_pend[i])
        acc = acc_ref[i]
        o_s = acc[0:HEAD_DIM] * (1.0 / acc[HEAD_DIM:HEAD_DIM + 1])
        o_c = oc_ref[g, :, h * T:(h + 1) * T]
        gate = [g_ref[0, pl.ds(i * N_BRANCH + br, 1), :] for br in range(N_BRANCH)]
        out = gate[0] * o_c + gate[1] * o_s + gate[2] * o_w[i]
        o_ref[0, i * HEAD_DIM:(i + 1) * HEAD_DIM, :] = out.astype(BF16)


def _nsa_call(qT, gT, kc, vcT, ks, vsT, kw, vwT, consts):
    B, _, S = qT.shape
    G = N_KV
    T = Q_TILE
    C = KEY_CHUNK
    n_cmp = kc.shape[2]
    rows = G * HPG * HEAD_DIM
    n_cols = G * HPG
    e_mat, kdead, movT, cbias, wbias, tbias = consts
    once = pl.Buffered(1)
    per_batch = lambda shape: pl.BlockSpec((1,) + shape, lambda b, i: (b, 0, 0, 0), pipeline_mode=once)
    full = lambda a: pl.BlockSpec(a.shape, lambda b, i: (0, 0), pipeline_mode=once)
    in_specs = [
        pl.BlockSpec((1, rows, T), lambda b, i: (b, 0, i)),
        pl.BlockSpec((1, 32, T), lambda b, i: (b, 0, i)),
        per_batch((G, n_cmp, LANES)),
        per_batch((G, HEAD_DIM, n_cmp)),
        per_batch((G, S, LANES)),
        per_batch((S // C, VT_ROWS, C)),
        per_batch((G, S, LANES)),
        per_batch((S // WIN_UNIT, VT_ROWS, WIN_UNIT)),
        full(e_mat), full(kdead), full(movT), full(cbias), full(wbias), full(tbias),
    ]
    return pl.pallas_call(
        _nsa_kernel, grid=(B, S // T), in_specs=in_specs,
        out_specs=pl.BlockSpec((1, rows, T), lambda b, i: (b, 0, i)),
        out_shape=jax.ShapeDtypeStruct((B, rows, S), BF16),
        scratch_shapes=[pltpu.VMEM((n_cols, 2 * LANES, T), BF16),
                        pltpu.VMEM((n_cols, C, T), F32), pltpu.VMEM((n_cols, C, T), F32),
                        pltpu.VMEM((n_cols, C, T), BF16), pltpu.VMEM((n_cols, C, T), BF16),
                        pltpu.VMEM((n_cols, VT_GROUP_ROWS, T), F32),
                        pltpu.VMEM((G, HEAD_DIM, HPG * T), F32),
                        pltpu.VMEM((G, S // SLC_BLOCK, T), F32)],
        name="nsa",
        compiler_params=pltpu.CompilerParams(
            dimension_semantics=("parallel", "arbitrary"),
            vmem_limit_bytes=VMEM_LIMIT),
    )(qT, gT, kc, vcT, ks, vsT, kw, vwT, e_mat, kdead, movT, cbias, wbias, tbias)


def _mix_ffn_kernel(yc_ref, ynT_ref, x_ref, p_ref, wc_ref, wn_ref, bo_ref, g1_ref, beta1_ref,
                    wup_ref, bup_ref, wdn_ref, bdn_ref, wpe_ref, wpg_ref, g2_ref, beta2_ref,
                    o_ref, acc_ref):
    mix = _dot(yc_ref[0], wc_ref[...]) + _dot_tn(ynT_ref[0], wn_ref[...])
    x1 = _layer_norm(ALPHA * x_ref[0] + mix + bo_ref[...], g1_ref[...], beta1_ref[...])
    xb = x1.astype(BF16)
    ple = _dot(p_ref[0].astype(BF16), wpe_ref[...]) * jax.nn.sigmoid(_dot(xb, wpg_ref[...]))
    acc_ref[...] = ALPHA * x1 + ple + bdn_ref[...]
    for c in range(0, D_FF, FFN_CHUNK):
        u = _dot(xb, wup_ref[:, c:c + FFN_CHUNK]) + bup_ref[:, c:c + FFN_CHUNK]
        u = jnp.square(jnp.maximum(u, 0.0)).astype(BF16)
        acc_ref[...] += _dot(u, wdn_ref[c:c + FFN_CHUNK, :])
    o_ref[0] = _layer_norm(acc_ref[...], g2_ref[...], beta2_ref[...])


def _mix_ffn_call(yc, ynT, x, p, wc, wn, bo, g1, beta1, wup, bup, wdn, bdn, wpe, wpg, g2, beta2):
    B, S, D = x.shape
    tm = FFN_ROWS
    const = lambda b, i: (0, 0)
    rows = lambda width: pl.BlockSpec((1, tm, width), lambda b, i: (b, i, 0))
    weight = lambda shape: pl.BlockSpec(shape, const, pipeline_mode=pl.Buffered(1))
    vec = pl.BlockSpec((1, D), const)
    return pl.pallas_call(
        _mix_ffn_kernel, grid=(B, S // tm),
        in_specs=[
            rows(CONV_CH),
            pl.BlockSpec((1, D - CONV_CH, tm), lambda b, i: (b, 0, i)),
            rows(D), rows(PLE_DIM),
            weight((CONV_CH, D)), weight((D - CONV_CH, D)), vec, vec, vec,
            weight((D, D_FF)), pl.BlockSpec((1, D_FF), const), weight((D_FF, D)), vec,
            weight((PLE_DIM, D)), weight((D, D)), vec, vec,
        ],
        out_specs=rows(D),
        out_shape=jax.ShapeDtypeStruct((B, S, D), F32),
        scratch_shapes=[pltpu.VMEM((tm, D), F32)],
        name="mixffn",
        compiler_params=pltpu.CompilerParams(
            dimension_semantics=("parallel", "parallel"), vmem_limit_bytes=VMEM_LIMIT),
    )(yc, ynT, x, p, wc, wn, bo, g1, beta1, wup, bup, wdn, bdn, wpe, wpg, g2, beta2)


def _pad_groups(w):
    z = jnp.zeros(w.shape[:-1] + (HEAD_DIM,), w.dtype)
    return jnp.concatenate([w[..., :HEAD_DIM], z, w[..., HEAD_DIM:], z], axis=-1)


def _overlap_matrix_t(n_cmp_padded, n_slc):
    n_cmp = n_cmp_padded - 1
    c0 = np.arange(n_cmp) * CMP_STRIDE
    c1 = c0 + CMP_LEN - 1
    s0 = np.arange(n_slc) * SLC_BLOCK
    s1 = s0 + SLC_BLOCK - 1
    m = ((c0[:, None] <= s1[None, :]) & (c1[:, None] >= s0[None, :])).astype(np.float32)
    out = np.zeros((n_slc, n_cmp_padded), np.float32)
    out[:, :n_cmp] = m.T
    return out


def _vt_rows(w, bias):
    d = w.shape[0]
    pad = VT_GROUP_ROWS - HEAD_DIM
    one = jnp.zeros((pad,), F32).at[0].set(1.0)
    ws, bs = [], []
    for g in range(N_KV):
        ws += [w[:, g * HEAD_DIM:(g + 1) * HEAD_DIM], jnp.zeros((d, pad), F32)]
        bs += [bias[g * HEAD_DIM:(g + 1) * HEAD_DIM], one]
    return jnp.concatenate(ws, axis=1), jnp.concatenate(bs)


def _nsa_constants(S, n_cmp_p):
    T, C = Q_TILE, KEY_CHUNK
    t = np.arange(T)[None, :]
    neg = np.float32(NEG_INF)

    def bias(valid):
        return jnp.asarray(np.where(valid, np.float32(0.0), neg))

    e_mat = (np.arange(S)[:, None] // SLC_BLOCK == np.arange(S // SLC_BLOCK)[None, :])
    kdead = np.zeros((C, LANES), np.float32)
    kdead[:, HEAD_DIM] = 1.0
    j = np.arange(2 * n_cmp_p)[:, None]
    cbias = bias(CMP_STRIDE * (j - n_cmp_p) + CMP_LEN - 1 <= t)
    j = np.arange(2 * WINDOW + T)[:, None]
    wbias = bias((t < j) & (j <= WINDOW + t))
    j = np.arange(2 * C - T)[:, None]
    tbias = bias(j - (C - T) <= t)
    return (jnp.asarray(e_mat.astype(np.float32), BF16), jnp.asarray(kdead, BF16),
            jnp.asarray(_overlap_matrix_t(n_cmp_p, S // SLC_BLOCK), BF16), cbias, wbias, tbias)


def _layer(x, p, w_in, b_in, conv_dw_w, conv_dw_b, conv_ln_g, conv_ln_b,
           cmp_pos_k, cmp_w1_k, cmp_w2_k, cmp_pos_v, cmp_w1_v, cmp_w2_v,
           w_out, b_out, ln1_g, ln1_b, w_up, b_up, w_down, b_down, w_pe, w_pg, ln2_g, ln2_b):
    B, S, D = x.shape
    scale = HEAD_DIM ** -0.5 * LOG2_E
    row = lambda v: v.reshape(1, -1).astype(F32)

    w, bias = w_in, b_in
    wrm = jnp.concatenate(
        [w[:, 0:1024], w[:, 1536:1792], _pad_groups(w[:, 1792:1920]), _pad_groups(w[:, 2048:2176])],
        axis=1).astype(BF16)
    brm = row(jnp.concatenate(
        [bias[0:1024], bias[1536:1792], _pad_groups(bias[1792:1920]), _pad_groups(bias[2048:2176])]))
    w_vs, b_vs = _vt_rows(w[:, 1920:2048], bias[1920:2048])
    w_vw, b_vw = _vt_rows(w[:, 2176:2304], bias[2176:2304])
    wt = jnp.concatenate(
        [w[:, 1024:1536] * scale, w_vs, w_vw, w[:, 2304:2328], jnp.zeros((D, 8), F32)],
        axis=1).T.astype(BF16)
    bt = jnp.concatenate(
        [bias[1024:1536] * scale, b_vs, b_vw, bias[2304:2328], jnp.zeros((8,), F32)]).reshape(-1, 1)

    conv_w = jnp.broadcast_to(conv_dw_w.reshape(CONV_WIDTH, 1, CONV_CH),
                              (CONV_WIDTH, SUBLANES, CONV_CH))
    y_conv, kc, vc, ks, kw, qT, vsT, vwT, gT = _proj_call(
        x, wrm, brm, wt, bt, conv_w, row(conv_dw_b), row(conv_ln_g), row(conv_ln_b))

    n_cmp_p = S // CMP_STRIDE
    blk_w = CMP_STRIDE * HEAD_DIM
    w2k = jnp.concatenate([cmp_w2_k, jnp.zeros((CMP_HIDDEN, LANES - HEAD_DIM), F32)], axis=1)
    kcz, vcT = _compress_call(
        kc, vc, cmp_pos_k.reshape(2, blk_w), cmp_w1_k.astype(BF16), w2k.astype(BF16),
        cmp_pos_v.reshape(2, blk_w), cmp_w1_v.astype(BF16), cmp_w2_v.T.astype(BF16))

    y_nsaT = _nsa_call(qT, gT, kcz, vcT, ks, vsT, kw, vwT, _nsa_constants(S, n_cmp_p))

    wo = w_out.astype(BF16)
    return _mix_ffn_call(
        y_conv, y_nsaT, x, p, wo[:CONV_CH], wo[CONV_CH:], row(b_out), row(ln1_g), row(ln1_b),
        w_up.astype(BF16), row(b_up), w_down.astype(BF16), row(b_down),
        w_pe.astype(BF16), w_pg.astype(BF16), row(ln2_g), row(ln2_b))


def kernel(x, p, w_in, b_in, conv_dw_w, conv_dw_b, conv_ln_g, conv_ln_b, cmp_pos_k, cmp_w1_k, cmp_w2_k, cmp_pos_v, cmp_w1_v, cmp_w2_v, w_out, b_out, ln1_g, ln1_b, w_up, b_up, w_down, b_down, w_pe, w_pg, ln2_g, ln2_b):
    params = (w_in, b_in, conv_dw_w, conv_dw_b, conv_ln_g, conv_ln_b,
              cmp_pos_k, cmp_w1_k, cmp_w2_k, cmp_pos_v, cmp_w1_v, cmp_w2_v,
              w_out, b_out, ln1_g, ln1_b, w_up, b_up, w_down, b_down, w_pe, w_pg, ln2_g, ln2_b)
    for i in range(DEPTH):
        x = _layer(x, p[i], *[t[i] for t in params])
    return x
```
